```python
import functools
import jax, jax.numpy as jnp
from jax import lax
import numpy as np

D_MODEL = 2048
BATCH = 4
SEQ = 2048
DEPTH = 2
DEC_BATCH = 32
DEC_SEQ = 8
PAST_LEN = 8192
PAGE_SIZE = 128

N_HEADS = 16
HEAD_DIM = D_MODEL // N_HEADS
D_ATTN = N_HEADS * HEAD_DIM
D_FF = ((8 * D_MODEL // 3 + 127) // 128) * 128
D_CONV = D_MODEL
CONV_WIDTH = 31
Q_BLOCK = 128
N_A_LAYERS = DEPTH // 2
N_B_LAYERS = DEPTH - N_A_LAYERS
N_NORMS = 6
RMS_EPS = 1e-6
LN_EPS = 1e-5
FORGET_BIAS_INIT = 3.0
ATTN_SCALE = HEAD_DIM ** -0.5
NEG_INF = -1e30

kernel_name = "yoco_conformer_conv_fox_step"


def rmsnorm(x, g):
    xf = x.astype(jnp.float32)
    y = xf * lax.rsqrt(jnp.mean(xf * xf, axis=-1, keepdims=True) + RMS_EPS)
    return y.astype(x.dtype) * g


def layernorm(x, g, b):
    xf = x.astype(jnp.float32)
    mu = jnp.mean(xf, axis=-1, keepdims=True)
    var = jnp.mean(jnp.square(xf - mu), axis=-1, keepdims=True)
    return ((xf - mu) * lax.rsqrt(var + LN_EPS)).astype(x.dtype) * g + b


def swiglu(x, w_gate, w_up, w_down):
    return (jax.nn.silu(x @ w_gate) * (x @ w_up)) @ w_down


def conformer_conv(x, pad, w_pw1, b_pw1, w_dw, b_dw, ln_g, ln_b, w_pw2, b_pw2):
    a, gate = jnp.split(x @ w_pw1 + b_pw1, 2, axis=-1)
    u = a * jax.nn.sigmoid(gate)
    u_full = jnp.concatenate([pad.astype(u.dtype), u], axis=1)
    y = lax.conv_general_dilated(
        u_full, w_dw[:, None, :].astype(u.dtype), window_strides=(1,), padding='VALID',
        dimension_numbers=('NWC', 'WIO', 'NWC'), feature_group_count=D_CONV) + b_dw
    y = jax.nn.silu(layernorm(y, ln_g, ln_b))
    return y @ w_pw2 + b_pw2, u_full[:, -(CONV_WIDTH - 1):]


def shared_kv(h, g, w_k, w_v, w_f, b_f):
    B, T, _ = h.shape
    z = rmsnorm(h, g)
    k = (z @ w_k).reshape(B, T, N_HEADS, HEAD_DIM)
    v = (z @ w_v).reshape(B, T, N_HEADS, HEAD_DIM)
    logf = jax.nn.log_sigmoid((z @ w_f + b_f).astype(jnp.float32))
    return k, v, logf


def fox_attention_prompt(q, k, v, logf):
    B, T, H, Dh = q.shape
    c = jnp.cumsum(logf.astype(jnp.float32), axis=1).transpose(0, 2, 1)
    pos = jnp.arange(T)
    qb = Q_BLOCK if T % Q_BLOCK == 0 else T
    nb = T // qb
    q_blk = q.reshape(B, nb, qb, H, Dh).transpose(1, 0, 2, 3, 4)
    c_blk = c.reshape(B, H, nb, qb).transpose(2, 0, 1, 3)
    p_blk = pos.reshape(nb, qb)

    def one_block(args):
        qi, ci, pi = args
        s = jnp.einsum('bqhd,bkhd->bhqk', qi, k).astype(jnp.float32) * ATTN_SCALE
        s = s + ci[..., :, None] - c[:, :, None, :]
        s = jnp.where(pos[None, :] <= pi[:, None], s, NEG_INF)
        p = jax.nn.softmax(s, axis=-1)
        return jnp.einsum('bhqk,bkhd->bqhd', p.astype(v.dtype), v)

    o = lax.map(one_block, (q_blk, c_blk, p_blk))
    return o.transpose(1, 0, 2, 3, 4).reshape(B, T, H * Dh)


def fox_attention_sample(q, k_new, v_new, logf_new, k_past, v_past, logf_past):
    B, T, H, Dh = q.shape
    P = k_past.shape[1]
    c = jnp.cumsum(jnp.concatenate([logf_past.astype(jnp.float32),
                                    logf_new.astype(jnp.float32)], axis=1), axis=1)
    c = c.transpose(0, 2, 1)
    cq = c[:, :, P:]
    s = jnp.concatenate([jnp.einsum('bqhd,bkhd->bhqk', q, k_past),
                         jnp.einsum('bqhd,bkhd->bhqk', q, k_new)], axis=-1)
    s = s.astype(jnp.float32) * ATTN_SCALE + cq[..., :, None] - c[:, :, None, :]
    q_pos = P + jnp.arange(T)
    k_pos = jnp.arange(P + T)
    s = jnp.where(k_pos[None, :] <= q_pos[:, None], s, NEG_INF)
    p = jax.nn.softmax(s, axis=-1).astype(v_new.dtype)
    o = (jnp.einsum('bhqk,bkhd->bqhd', p[..., :P], v_past)
         + jnp.einsum('bhqk,bkhd->bqhd', p[..., P:], v_new))
    return o.reshape(B, T, H * Dh)


def run_trunk(x, conv_pads, attend, p):
    B, T, _ = x.shape
    h = x
    new_conv = []
    kv = None
    for i in range(DEPTH):
        g = p['norm_gain'][i]
        h = h + 0.5 * rmsnorm(swiglu(rmsnorm(h, g[0]), p['ffn1_w_gate'][i], p['ffn1_w_up'][i],
                                     p['ffn1_w_down'][i]), g[1])
        if i < N_A_LAYERS:
            m, st = conformer_conv(rmsnorm(h, g[2]), conv_pads[i], p['conv_w_pw1'][i],
                                   p['conv_b_pw1'][i], p['conv_w_dw'][i], p['conv_b_dw'][i],
                                   p['conv_ln_g'][i], p['conv_ln_b'][i], p['conv_w_pw2'][i],
                                   p['conv_b_pw2'][i])
            new_conv.append(st)
        else:
            j = i - N_A_LAYERS
            q = (rmsnorm(h, g[2]) @ p['attn_w_q'][j]).reshape(B, T, N_HEADS, HEAD_DIM)
            m = attend(q, kv[0], kv[1], kv[2]) @ p['attn_w_o'][j]
        h = h + rmsnorm(m, g[3])
        h = h + 0.5 * rmsnorm(swiglu(rmsnorm(h, g[4]), p['ffn2_w_gate'][i], p['ffn2_w_up'][i],
                                     p['ffn2_w_down'][i]), g[5])
        if i == N_A_LAYERS - 1:
            kv = shared_kv(h, p['kv_norm_g'], p['w_k'], p['w_v'], p['w_fgate'], p['b_fgate'])
    return h, jnp.stack(new_conv, axis=0), kv


def setup_inputs(seed: int = 0) -> dict:
    key = jax.random.key(seed)
    ks = iter(jax.random.split(key, 40))
    n_pages = PAST_LEN // PAGE_SIZE
    n_used = DEC_BATCH * n_pages
    n_phys = n_used + (n_used + 3) // 4

    def nrm(shape, scale):
        return jax.random.normal(next(ks), shape, jnp.float32) * scale

    d, na, nbl = D_MODEL, N_A_LAYERS, N_B_LAYERS
    return {
        'x_prompt': nrm((BATCH, SEQ, d), 1.0),
        'x_sample': nrm((DEC_BATCH, DEC_SEQ, d), 1.0),
        'state_conv': nrm((na, DEC_BATCH, CONV_WIDTH - 1, D_CONV), 0.5),
        'cache_k': nrm((n_phys, PAGE_SIZE, N_HEADS, HEAD_DIM), 1.0),
        'cache_v': nrm((n_phys, PAGE_SIZE, N_HEADS, HEAD_DIM), 1.0),
        'cache_logf': jax.nn.log_sigmoid(FORGET_BIAS_INIT + nrm((n_phys, PAGE_SIZE, N_HEADS), 1.0)),
        'page_table': jax.random.permutation(next(ks), n_phys)[:n_used].reshape(
            DEC_BATCH, n_pages).astype(jnp.int32),
        'norm_gain': 1.0 + nrm((DEPTH, N_NORMS, d), 0.02),
        'ffn1_w_gate': nrm((DEPTH, d, D_FF), d ** -0.5),
        'ffn1_w_up': nrm((DEPTH, d, D_FF), d ** -0.5),
        'ffn1_w_down': nrm((DEPTH, D_FF, d), D_FF ** -0.5),
        'ffn2_w_gate': nrm((DEPTH, d, D_FF), d ** -0.5),
        'ffn2_w_up': nrm((DEPTH, d, D_FF), d ** -0.5),
        'ffn2_w_down': nrm((DEPTH, D_FF, d), D_FF ** -0.5),
        'conv_w_pw1': nrm((na, d, 2 * D_CONV), d ** -0.5),
        'conv_b_pw1': nrm((na, 2 * D_CONV), 0.02),
        'conv_w_dw': nrm((na, CONV_WIDTH, D_CONV), CONV_WIDTH ** -0.5),
        'conv_b_dw': nrm((na, D_CONV), 0.02),
        'conv_ln_g': 1.0 + nrm((na, D_CONV), 0.02),
        'conv_ln_b': nrm((na, D_CONV), 0.02),
        'conv_w_pw2': nrm((na, D_CONV, d), D_CONV ** -0.5),
        'conv_b_pw2': nrm((na, d), 0.02),
        'kv_norm_g': 1.0 + nrm((d,), 0.02),
        'w_k': nrm((d, D_ATTN), d ** -0.5),
        'w_v': nrm((d, D_ATTN), d ** -0.5),
        'w_fgate': nrm((d, N_HEADS), d ** -0.5),
        'b_fgate': FORGET_BIAS_INIT + nrm((N_HEADS,), 0.1),
        'attn_w_q': nrm((nbl, d, D_ATTN), d ** -0.5),
        'attn_w_o': nrm((nbl, D_ATTN, d), D_ATTN ** -0.5),
    }


def reference(x_prompt, x_sample, state_conv, cache_k, cache_v, cache_logf, page_table,
              norm_gain, ffn1_w_gate, ffn1_w_up, ffn1_w_down, ffn2_w_gate, ffn2_w_up,
              ffn2_w_down, conv_w_pw1, conv_b_pw1, conv_w_dw, conv_b_dw, conv_ln_g, conv_ln_b,
              conv_w_pw2, conv_b_pw2, kv_norm_g, w_k, w_v, w_fgate, b_fgate, attn_w_q,
              attn_w_o):
    params = {
        'norm_gain': norm_gain, 'ffn1_w_gate': ffn1_w_gate, 'ffn1_w_up': ffn1_w_up,
        'ffn1_w_down': ffn1_w_down, 'ffn2_w_gate': ffn2_w_gate, 'ffn2_w_up': ffn2_w_up,
        'ffn2_w_down': ffn2_w_down, 'conv_w_pw1': conv_w_pw1, 'conv_b_pw1': conv_b_pw1,
        'conv_w_dw': conv_w_dw, 'conv_b_dw': conv_b_dw, 'conv_ln_g': conv_ln_g,
        'conv_ln_b': conv_ln_b, 'conv_w_pw2': conv_w_pw2, 'conv_b_pw2': conv_b_pw2,
        'kv_norm_g': kv_norm_g, 'w_k': w_k, 'w_v': w_v, 'w_fgate': w_fgate,
        'b_fgate': b_fgate, 'attn_w_q': attn_w_q, 'attn_w_o': attn_w_o,
    }
    b_p = x_prompt.shape[0]
    zero_pads = [jnp.zeros((b_p, CONV_WIDTH - 1, D_CONV), x_prompt.dtype)
                 for _ in range(N_A_LAYERS)]
    y_prompt, conv_p, kv_p = run_trunk(x_prompt, zero_pads, fox_attention_prompt, params)

    b_s, n_pages = page_table.shape
    past = n_pages * PAGE_SIZE
    k_past = cache_k[page_table].reshape(b_s, past, N_HEADS, HEAD_DIM)
    v_past = cache_v[page_table].reshape(b_s, past, N_HEADS, HEAD_DIM)
    logf_past = cache_logf[page_table].reshape(b_s, past, N_HEADS)
    attend_s = functools.partial(fox_attention_sample, k_past=k_past, v_past=v_past,
                                 logf_past=logf_past)
    pads_s = [state_conv[i] for i in range(N_A_LAYERS)]
    y_sample, conv_s, kv_s = run_trunk(x_sample, pads_s, attend_s, params)

    return (y_prompt, y_sample, conv_p, kv_p[0], kv_p[1], kv_p[2],
            conv_s, kv_s[0], kv_s[1], kv_s[2])
```

```python
import functools

import jax
import jax.numpy as jnp
from jax import lax
from jax.experimental import pallas as pl
from jax.experimental.pallas import tpu as pltpu

F32 = jnp.float32
BF16 = jnp.bfloat16

RMS_EPS = 1e-6
LN_EPS = 1e-5
NEG_INF = -1e30
N_NORMS = 6
PAGES_PER_STEP = 4

VMEM_LIMIT_BYTES = 56 * 1024 * 1024

_NT = (((1,), (1,)), ((), ()))
_TN = (((0,), (0,)), ((), ()))


def _params(semantics):
    return pltpu.CompilerParams(dimension_semantics=semantics, vmem_limit_bytes=VMEM_LIMIT_BYTES)


def _rms(x, g):
    ms = jnp.mean(x * x, axis=-1, keepdims=True)
    return (x * lax.rsqrt(ms + RMS_EPS)) * g


def _dot(a, b):
    return jnp.dot(a, b, preferred_element_type=F32)


def _split3(x):
    hi = x.astype(BF16)
    r1 = x - hi.astype(F32)
    mid = r1.astype(BF16)
    lo = (r1 - mid.astype(F32)).astype(BF16)
    return hi, mid, lo


def _ones_where(mask):
    return jnp.where(mask, 1.0, 0.0).astype(BF16)


def _ffn_body(x_ref, gpre_ref, gpost_ref, wg_ref, wu_ref, wd_ref, o_ref, xn_ref, *, d_ff, tf):
    f = pl.program_id(1)

    @pl.when(f == 0)
    def _():
        xn_ref[...] = _rms(x_ref[...], gpre_ref[...]).astype(BF16)

    xn = xn_ref[...]
    gate = _dot(xn, wg_ref[...].astype(BF16))
    up = _dot(xn, wu_ref[...].astype(BF16))
    act = gate * jax.nn.sigmoid(gate) * up
    col = f * tf + lax.broadcasted_iota(jnp.int32, (1, tf), 1)
    act = jnp.where(col < d_ff, act, 0.0).astype(BF16)
    row = f * tf + lax.broadcasted_iota(jnp.int32, (tf, 1), 0)
    wd = jnp.where(row < d_ff, wd_ref[...], 0.0).astype(BF16)
    contrib = _dot(act, wd)

    @pl.when(f == 0)
    def _():
        o_ref[...] = contrib

    @pl.when(f > 0)
    def _():
        o_ref[...] += contrib

    @pl.when(f == pl.num_programs(1) - 1)
    def _():
        o_ref[...] = x_ref[...] + 0.5 * _rms(o_ref[...], gpost_ref[...])


def _ffn(h, gains, n_pre, n_post, w_gate, w_up, w_down, layer, *, tm, tf):
    m, d = h.shape
    d_ff = w_gate.shape[-1]
    grid = (m // tm, pl.cdiv(d_ff, tf))
    one = pl.Buffered(1)
    return pl.pallas_call(
        functools.partial(_ffn_body, d_ff=d_ff, tf=tf),
        out_shape=jax.ShapeDtypeStruct((m, d), F32),
        grid=grid,
        in_specs=[
            pl.BlockSpec((tm, d), lambda i, f: (i, 0), pipeline_mode=one),
            pl.BlockSpec((None, 1, d), lambda i, f: (n_pre, 0, 0)),
            pl.BlockSpec((None, 1, d), lambda i, f: (n_post, 0, 0)),
            pl.BlockSpec((None, d, tf), lambda i, f: (layer, 0, f)),
            pl.BlockSpec((None, d, tf), lambda i, f: (layer, 0, f)),
            pl.BlockSpec((None, tf, d), lambda i, f: (layer, f, 0)),
        ],
        out_specs=pl.BlockSpec((tm, d), lambda i, f: (i, 0), pipeline_mode=one),
        scratch_shapes=[pltpu.VMEM((tm, d), BF16)],
        compiler_params=_params(("parallel", "arbitrary")),
        name="ffn",
    )(h, gains, gains, w_gate, w_up, w_down)


def _glu_body(x_ref, g_ref, wa_ref, wg_ref, ba_ref, bg_ref, o_ref, xn_ref):
    @pl.when(pl.program_id(1) == 0)
    def _():
        xn_ref[...] = _rms(x_ref[...], g_ref[...]).astype(BF16)

    xn = xn_ref[...]
    a = _dot(xn, wa_ref[...].astype(BF16)) + ba_ref[...]
    gate = _dot(xn, wg_ref[...].astype(BF16)) + bg_ref[...]
    o_ref[...] = a * jax.nn.sigmoid(gate)


def _glu_proj(h, gains, n_gain, w_pw1, b_pw1, *, tm, tn):
    m, d = h.shape
    dc = w_pw1.shape[-1] // 2
    nj = dc // tn
    b3 = b_pw1.reshape(b_pw1.shape[0], 1, 2 * dc)
    return pl.pallas_call(
        _glu_body,
        out_shape=jax.ShapeDtypeStruct((m, dc), F32),
        grid=(m // tm, nj),
        in_specs=[
            pl.BlockSpec((tm, d), lambda i, j: (i, 0), pipeline_mode=pl.Buffered(1)),
            pl.BlockSpec((None, 1, d), lambda i, j: (n_gain, 0, 0)),
            pl.BlockSpec((None, d, tn), lambda i, j: (0, 0, j)),
            pl.BlockSpec((None, d, tn), lambda i, j: (0, 0, j + nj)),
            pl.BlockSpec((None, 1, tn), lambda i, j: (0, 0, j)),
            pl.BlockSpec((None, 1, tn), lambda i, j: (0, 0, j + nj)),
        ],
        out_specs=pl.BlockSpec((tm, tn), lambda i, j: (i, j)),
        scratch_shapes=[pltpu.VMEM((tm, d), BF16)],
        compiler_params=_params(("parallel", "arbitrary")),
        name="glu_proj",
    )(h, gains, w_pw1, w_pw1, b3, b3)


def _kv_body(x_ref, g_ref, wk_ref, wv_ref, wf_ref, bf_ref, k_ref, v_ref, lf_ref, xn_ref):
    @pl.when(pl.program_id(1) == 0)
    def _():
        xn = _rms(x_ref[...], g_ref[...]).astype(BF16)
        xn_ref[...] = xn
        z = _dot(xn, wf_ref[...].astype(BF16)) + bf_ref[...]
        lf_ref[...] = jnp.minimum(z, 0.0) - jnp.log1p(jnp.exp(-jnp.abs(z)))

    xn = xn_ref[...]
    k_ref[...] = _dot(xn, wk_ref[...].astype(BF16))
    v_ref[...] = _dot(xn, wv_ref[...].astype(BF16))


def _kv_proj(h, g, w_k, w_v, w_f, b_f, *, tm, tn):
    m, d = h.shape
    da = w_k.shape[-1]
    nh = w_f.shape[-1]
    return pl.pallas_call(
        _kv_body,
        out_shape=(jax.ShapeDtypeStruct((m, da), F32), jax.ShapeDtypeStruct((m, da), F32),
                   jax.ShapeDtypeStruct((m, nh), F32)),
        grid=(m // tm, da // tn),
        in_specs=[
            pl.BlockSpec((tm, d), lambda i, j: (i, 0), pipeline_mode=pl.Buffered(1)),
            pl.BlockSpec((1, d), lambda i, j: (0, 0)),
            pl.BlockSpec((d, tn), lambda i, j: (0, j)),
            pl.BlockSpec((d, tn), lambda i, j: (0, j)),
            pl.BlockSpec((d, nh), lambda i, j: (0, 0)),
            pl.BlockSpec((1, nh), lambda i, j: (0, 0)),
        ],
        out_specs=(pl.BlockSpec((tm, tn), lambda i, j: (i, j)),
                   pl.BlockSpec((tm, tn), lambda i, j: (i, j)),
                   pl.BlockSpec((tm, nh), lambda i, j: (i, 0))),
        scratch_shapes=[pltpu.VMEM((tm, d), BF16)],
        compiler_params=_params(("parallel", "arbitrary")),
        name="kv_proj",
    )(h, g.reshape(1, d), w_k, w_v, w_f, b_f.reshape(1, nh))


def _q_body(x_ref, g_ref, w_ref, o_ref, xn_ref):
    @pl.when(pl.program_id(1) == 0)
    def _():
        xn_ref[...] = _rms(x_ref[...], g_ref[...]).astype(BF16)

    o_ref[...] = _dot(xn_ref[...], w_ref[...].astype(BF16)).astype(o_ref.dtype)


def _q_proj(h, gains, n_gain, w_q, *, tm, tn):
    m, d = h.shape
    da = w_q.shape[-1]
    return pl.pallas_call(
        _q_body,
        out_shape=jax.ShapeDtypeStruct((m, da), BF16),
        grid=(m // tm, da // tn),
        in_specs=[
            pl.BlockSpec((tm, d), lambda i, j: (i, 0), pipeline_mode=pl.Buffered(1)),
            pl.BlockSpec((None, 1, d), lambda i, j: (n_gain, 0, 0)),
            pl.BlockSpec((None, d, tn), lambda i, j: (0, 0, j)),
        ],
        out_specs=pl.BlockSpec((tm, tn), lambda i, j: (i, j)),
        scratch_shapes=[pltpu.VMEM((tm, d), BF16)],
        compiler_params=_params(("parallel", "arbitrary")),
        name="q_proj",
    )(h, gains, w_q)


def _resid_mm_body(h_ref, y_ref, w_ref, b_ref, g_ref, o_ref):
    k = pl.program_id(1)
    contrib = _dot(y_ref[...], w_ref[...].astype(BF16))

    @pl.when(k == 0)
    def _():
        o_ref[...] = contrib + b_ref[...]

    @pl.when(k > 0)
    def _():
        o_ref[...] += contrib

    @pl.when(k == pl.num_programs(1) - 1)
    def _():
        o_ref[...] = h_ref[...] + _rms(o_ref[...], g_ref[...])


def _resid_mm(h, y, w, b, gains, n_gain, *, tm, tk):
    m, d = h.shape
    kdim = y.shape[-1]
    one = pl.Buffered(1)
    return pl.pallas_call(
        _resid_mm_body,
        out_shape=jax.ShapeDtypeStruct((m, d), F32),
        grid=(m // tm, kdim // tk),
        in_specs=[
            pl.BlockSpec((tm, d), lambda i, k: (i, 0), pipeline_mode=one),
            pl.BlockSpec((tm, tk), lambda i, k: (i, k)),
            pl.BlockSpec((None, tk, d), lambda i, k: (0, k, 0)),
            pl.BlockSpec((1, d), lambda i, k: (0, 0)),
            pl.BlockSpec((None, 1, d), lambda i, k: (n_gain, 0, 0)),
        ],
        out_specs=pl.BlockSpec((tm, d), lambda i, k: (i, 0), pipeline_mode=one),
        compiler_params=_params(("parallel", "arbitrary")),
        name="resid_mm",
    )(h, y, w, b.reshape(1, d), gains)


def _ln_silu(y, g, b):
    mu = jnp.mean(y, axis=-1, keepdims=True)
    dlt = y - mu
    var = jnp.mean(dlt * dlt, axis=-1, keepdims=True)
    z = (dlt * lax.rsqrt(var + LN_EPS)) * g + b
    return z * jax.nn.sigmoid(z)


_CONV_ROWS = 8
_CONV_COLS = 512
_HALO = 32


def _conv_prompt_body(halo_ref, u_ref, wdw_ref, bdw_ref, lng_ref, lnb_ref, o_ref, ubuf_ref, ybuf_ref,
                      *, tt, width):
    i = pl.program_id(1)
    d = u_ref.shape[-1]
    ubuf_ref[pl.ds(0, _HALO), :] = jnp.where(i > 0, halo_ref[...], 0.0)
    ubuf_ref[pl.ds(_HALO, tt), :] = u_ref[...]
    off = _HALO - (width - 1)

    def conv_chunk(r, carry):
        base = pl.multiple_of(r * _CONV_ROWS, _CONV_ROWS)
        for cb in range(d // _CONV_COLS):
            cs = slice(cb * _CONV_COLS, (cb + 1) * _CONV_COLS)
            blk = ubuf_ref[pl.ds(base, _CONV_ROWS + _HALO), cs]
            acc = jnp.broadcast_to(bdw_ref[:, cs], (_CONV_ROWS, _CONV_COLS))
            for w in range(width):
                acc = acc + blk[off + w:off + w + _CONV_ROWS] * wdw_ref[w:w + 1, cs]
            ybuf_ref[pl.ds(base, _CONV_ROWS), cs] = acc
        return carry

    lax.fori_loop(0, tt // _CONV_ROWS, conv_chunk, 0)

    def ln_chunk(r, carry):
        base = pl.multiple_of(r * 16, 16)
        y = ybuf_ref[pl.ds(base, 16), :]
        o_ref[pl.ds(base, 16), :] = _ln_silu(y, lng_ref[...], lnb_ref[...]).astype(o_ref.dtype)
        return carry

    lax.fori_loop(0, tt // 16, ln_chunk, 0)


def _conv_prompt(u, n_seq, seq_len, w_dw, b_dw, ln_g, ln_b, *, tt):
    d = u.shape[-1]
    width = w_dw.shape[1]
    nt = seq_len // tt
    hb = tt // _HALO
    return pl.pallas_call(
        functools.partial(_conv_prompt_body, tt=tt, width=width),
        out_shape=jax.ShapeDtypeStruct((n_seq * seq_len, d), BF16),
        grid=(n_seq, nt),
        in_specs=[
            pl.BlockSpec((_HALO, d), lambda b, i: (jnp.maximum((b * nt + i) * hb - 1, 0), 0)),
            pl.BlockSpec((tt, d), lambda b, i: (b * nt + i, 0)),
            pl.BlockSpec((None, width, d), lambda b, i: (0, 0, 0)),
            pl.BlockSpec((1, d), lambda b, i: (0, 0)),
            pl.BlockSpec((1, d), lambda b, i: (0, 0)),
            pl.BlockSpec((1, d), lambda b, i: (0, 0)),
        ],
        out_specs=pl.BlockSpec((tt, d), lambda b, i: (b * nt + i, 0)),
        scratch_shapes=[pltpu.VMEM((tt + _HALO, d), F32), pltpu.VMEM((tt, d), F32)],
        compiler_params=_params(("parallel", "parallel")),
        name="conv_prompt",
    )(u, u, w_dw, b_dw, ln_g, ln_b)


def _conv_sample_body(st_ref, u_ref, wdw_ref, bdw_ref, lng_ref, lnb_ref, y_ref, ns_ref, ubuf_ref,
                      *, ts, width):
    hist = width - 1
    ubuf_ref[pl.ds(0, hist), :] = st_ref[...]
    ubuf_ref[pl.ds(hist, ts), :] = u_ref[...]
    full = ubuf_ref[...]
    acc = jnp.broadcast_to(bdw_ref[...], (ts, full.shape[-1]))
    for w in range(width):
        acc = acc + full[w:w + ts] * wdw_ref[w:w + 1, :]
    y_ref[...] = _ln_silu(acc, lng_ref[...], lnb_ref[...])
    ns_ref[...] = full[ts:ts + hist]


def _conv_sample(u_s, state, w_dw, b_dw, ln_g, ln_b):
    bs, ts, d = u_s.shape
    width = w_dw.shape[1]
    hist = width - 1
    return pl.pallas_call(
        functools.partial(_conv_sample_body, ts=ts, width=width),
        out_shape=(jax.ShapeDtypeStruct((bs, ts, d), F32),
                   jax.ShapeDtypeStruct((1, bs, hist, d), F32)),
        grid=(bs,),
        in_specs=[
            pl.BlockSpec((None, None, hist, d), lambda b: (0, b, 0, 0)),
            pl.BlockSpec((None, ts, d), lambda b: (b, 0, 0)),
            pl.BlockSpec((None, width, d), lambda b: (0, 0, 0)),
            pl.BlockSpec((1, d), lambda b: (0, 0)),
            pl.BlockSpec((1, d), lambda b: (0, 0)),
            pl.BlockSpec((1, d), lambda b: (0, 0)),
        ],
        out_specs=(pl.BlockSpec((None, ts, d), lambda b: (b, 0, 0)),
                   pl.BlockSpec((None, None, hist, d), lambda b: (0, b, 0, 0))),
        scratch_shapes=[pltpu.VMEM((hist + ts, d), F32)],
        compiler_params=_params(("parallel",)),
        name="conv_sample",
    )(state, u_s, w_dw, b_dw, ln_g, ln_b)


def _cumsum_body(x_ref, c_ref, ct_ref, carry_ref, carryt_ref, *, tc):
    @pl.when(pl.program_id(1) == 0)
    def _():
        carry_ref[...] = jnp.zeros_like(carry_ref)
        carryt_ref[...] = jnp.zeros_like(carryt_ref)

    row = lax.broadcasted_iota(jnp.int32, (tc, tc), 0)
    col = lax.broadcasted_iota(jnp.int32, (tc, tc), 1)
    lower = _ones_where(col <= row)
    upper = _ones_where(row <= col)
    cb = carry_ref[...]
    ctb = carryt_ref[...]
    for piece in _split3(x_ref[...]):
        cb = cb + _dot(lower, piece)
        ctb = ctb + lax.dot_general(piece, upper, _TN, preferred_element_type=F32)
    c_ref[...] = cb
    ct_ref[...] = ctb
    carry_ref[...] = cb[tc - 1:tc, :]
    carryt_ref[...] = ctb[:, tc - 1:tc]


def _cumsum_prompt(logf, n_seq, seq_len, *, tc):
    nh = logf.shape[-1]
    nt = seq_len // tc
    return pl.pallas_call(
        functools.partial(_cumsum_body, tc=tc),
        out_shape=(jax.ShapeDtypeStruct((n_seq * seq_len, nh), F32),
                   jax.ShapeDtypeStruct((n_seq, nh, seq_len), F32)),
        grid=(n_seq, nt),
        in_specs=[pl.BlockSpec((tc, nh), lambda b, j: (b * nt + j, 0))],
        out_specs=(pl.BlockSpec((tc, nh), lambda b, j: (b * nt + j, 0)),
                   pl.BlockSpec((None, nh, tc), lambda b, j: (b, 0, j))),
        scratch_shapes=[pltpu.VMEM((1, nh), F32), pltpu.VMEM((nh, 1), F32)],
        compiler_params=_params(("parallel", "arbitrary")),
        name="cumsum_prompt",
    )(logf)


def _attn_prompt_body(q_ref, k_ref, v_ref, c_ref, ct_ref, o_ref, m_ref, l_ref, acc_ref,
                      *, n_heads, dh, scale):
    qi = pl.program_id(1)
    ki = pl.program_id(2)
    tq = q_ref.shape[0]
    tk = k_ref.shape[0]

    @pl.when(ki == 0)
    def _():
        m_ref[...] = jnp.full_like(m_ref, NEG_INF)
        l_ref[...] = jnp.zeros_like(l_ref)
        acc_ref[...] = jnp.zeros_like(acc_ref)

    def step(diagonal):
        if diagonal:
            row = lax.broadcasted_iota(jnp.int32, (tq, tk), 0)
            col = lax.broadcasted_iota(jnp.int32, (tq, tk), 1)
            causal = col <= row
        for h in range(n_heads):
            sl = slice(h * dh, (h + 1) * dh)
            kh = k_ref[:, sl].astype(BF16)
            vh = v_ref[:, sl].astype(BF16)
            s = lax.dot_general(q_ref[:, sl], kh, _NT, preferred_element_type=F32) * scale
            s = s + c_ref[:, h:h + 1] - ct_ref[h:h + 1, :]
            if diagonal:
                s = jnp.where(causal, s, NEG_INF)
            m_prev = m_ref[h]
            m_new = jnp.maximum(m_prev, jnp.max(s, axis=-1, keepdims=True))
            alpha = jnp.exp(m_prev - m_new)
            p = jnp.exp(s - m_new)
            l_ref[h] = alpha * l_ref[h] + jnp.sum(p, axis=-1, keepdims=True)
            acc_ref[:, sl] = alpha * acc_ref[:, sl] + _dot(p.astype(BF16), vh)
            m_ref[h] = m_new

    @pl.when(ki < qi)
    def _():
        step(False)

    @pl.when(ki == qi)
    def _():
        step(True)
        for h in range(n_heads):
            sl = slice(h * dh, (h + 1) * dh)
            o_ref[:, sl] = (acc_ref[:, sl] / l_ref[h]).astype(o_ref.dtype)


def _attn_prompt(q, k, v, c, ct, n_seq, seq_len, n_heads, *, tq):
    d = q.shape[-1]
    dh = d // n_heads
    nq = seq_len // tq
    kv_map = lambda b, qi, ki: (b * nq + jnp.minimum(ki, qi), 0)
    return pl.pallas_call(
        functools.partial(_attn_prompt_body, n_heads=n_heads, dh=dh, scale=dh ** -0.5),
        out_shape=jax.ShapeDtypeStruct((n_seq * seq_len, d), BF16),
        grid=(n_seq, nq, nq),
        in_specs=[
            pl.BlockSpec((tq, d), lambda b, qi, ki: (b * nq + qi, 0)),
            pl.BlockSpec((tq, d), kv_map),
            pl.BlockSpec((tq, d), kv_map),
            pl.BlockSpec((tq, n_heads), lambda b, qi, ki: (b * nq + qi, 0)),
            pl.BlockSpec((None, n_heads, tq), lambda b, qi, ki: (b, 0, jnp.minimum(ki, qi))),
        ],
        out_specs=pl.BlockSpec((tq, d), lambda b, qi, ki: (b * nq + qi, 0)),
        scratch_shapes=[pltpu.VMEM((n_heads, tq, 1), F32), pltpu.VMEM((n_heads, tq, 1), F32),
                        pltpu.VMEM((tq, d), F32)],
        compiler_params=_params(("parallel", "parallel", "arbitrary")),
        name="attn_prompt",
    )(q, k, v, c, ct)


def _lane_iota(shape):
    return lax.broadcasted_iota(jnp.int32, shape, len(shape) - 1)


def _suffix_sums_page(x, n_heads):
    n_rows, n_lanes = x.shape
    lane = _lane_iota(x.shape)
    sub = lax.broadcasted_iota(jnp.int32, x.shape, 0)
    y = x
    sh = n_heads
    while sh < n_lanes:
        y = y + jnp.where(lane + sh < n_lanes, pltpu.roll(y, n_lanes - sh, 1), 0.0)
        sh *= 2
    z = jnp.where(lane < n_heads, y, 0.0)
    sh = n_heads
    while sh < n_lanes:
        z = z + pltpu.roll(z, sh, 1)
        sh *= 2
    zi = z
    sh = 1
    while sh < n_rows:
        zi = zi + jnp.where(sub + sh < n_rows, pltpu.roll(zi, n_rows - sh, 0), 0.0)
        sh *= 2
    return y + (zi - z), zi[0:1, :]


def _attn_sample_body(pt_ref, q_ref, kn_ref, vn_ref, lfn_ref, *refs, n_heads, ts, scale):
    npg = PAGES_PER_STEP
    k_refs = refs[0:npg]
    v_refs = refs[npg:2 * npg]
    lf_refs = refs[2 * npg:3 * npg]
    o_ref, mask_ref, m_ref, l_ref, acc_ref, tail_ref, cq_ref = refs[3 * npg:]
    del pt_ref
    j = pl.program_id(1)
    rows, dh = q_ref.shape
    page = k_refs[0].shape[0]
    n_keys = page * n_heads

    def online_update(s, v_bf16, first):
        if first:
            m_new = jnp.max(s, axis=-1, keepdims=True)
            p = jnp.exp(s - m_new)
            l_ref[...] = jnp.sum(p, axis=-1, keepdims=True)
            acc_ref[...] = _dot(p.astype(BF16), v_bf16)
        else:
            m_prev = m_ref[...]
            m_new = jnp.maximum(m_prev, jnp.max(s, axis=-1, keepdims=True))
            alpha = jnp.exp(m_prev - m_new)
            p = jnp.exp(s - m_new)
            l_ref[...] = alpha * l_ref[...] + jnp.sum(p, axis=-1, keepdims=True)
            acc_ref[...] = alpha * acc_ref[...] + _dot(p.astype(BF16), v_bf16)
        m_ref[...] = m_new

    @pl.when(j == 0)
    def _():
        r_i = lax.broadcasted_iota(jnp.int32, (rows, n_keys), 0)
        l_i = _lane_iota((rows, n_keys))
        mask_ref[...] = jnp.where(l_i % n_heads == r_i // ts, 0.0, NEG_INF)

        n_new = ts * n_heads
        cn = jnp.broadcast_to(lfn_ref[...], (8, n_new))
        lane8 = _lane_iota((8, n_new))
        sh = n_heads
        while sh < n_new:
            cn = cn + jnp.where(lane8 >= sh, pltpu.roll(cn, sh, 1), 0.0)
            sh *= 2
        cn_row = cn[0:1, :]
        r2 = lax.broadcasted_iota(jnp.int32, (rows, n_new), 0)
        l2 = _lane_iota((rows, n_new))
        own = l2 == (r2 % ts) * n_heads + r2 // ts
        cq = jnp.sum(jnp.where(own, jnp.broadcast_to(cn_row, (rows, n_new)), 0.0),
                     axis=-1, keepdims=True)
        cq_ref[...] = cq
        tail_ref[...] = jnp.zeros_like(tail_ref)

        s = lax.dot_general(q_ref[...], kn_ref[...].astype(BF16), _NT,
                            preferred_element_type=F32) * scale
        s = s + cq - cn_row
        valid = (l2 % n_heads == r2 // ts) & (l2 // n_heads <= r2 % ts)
        s = jnp.where(valid, s, NEG_INF)
        online_update(s, vn_ref[...].astype(BF16), first=True)

    tail = tail_ref[0:1, :]
    e_tiles = [None] * npg
    for i in reversed(range(npg)):
        x = lf_refs[i][...]
        incl, tot = _suffix_sums_page(x, n_heads)
        e_tiles[i] = incl - x + tail
        tail = tail + tot
    tail_ref[...] = jnp.broadcast_to(tail, tail_ref.shape)
    e_row = jnp.concatenate([e[a:a + 1, :] for e in e_tiles for a in range(e.shape[0])], axis=1)

    k4 = jnp.concatenate([r[...].reshape(n_keys, dh).astype(BF16) for r in k_refs], axis=0)
    v4 = jnp.concatenate([r[...].reshape(n_keys, dh).astype(BF16) for r in v_refs], axis=0)
    s = lax.dot_general(q_ref[...], k4, _NT, preferred_element_type=F32) * scale
    s = s + cq_ref[...] + e_row
    s = s + jnp.concatenate([mask_ref[...]] * npg, axis=1)
    online_update(s, v4, first=False)

    @pl.when(j == pl.num_programs(1) - 1)
    def _():
        o_ref[...] = acc_ref[...] / l_ref[...]


def _attn_sample(q2, kn2, vn2, lfn, cache_k, cache_v, cache_logf, page_table, n_heads, ts):
    bs, rows, dh = q2.shape
    n_phys, page = cache_k.shape[0], cache_k.shape[1]
    n_pages = page_table.shape[1]
    npg = PAGES_PER_STEP
    n_steps = n_pages // npg
    n_keys = page * n_heads
    lanes = kn2.shape[1]
    lfc = cache_logf.reshape(n_phys, n_keys // lanes, lanes)

    def page_map(i, nd):
        return lambda b, j, pt: (pt[b, (n_steps - 1 - j) * npg + i],) + (0,) * nd

    per_seq = lambda b, j, pt: (b, 0, 0)
    in_specs = [pl.BlockSpec((None, rows, dh), per_seq),
                pl.BlockSpec((None, lanes, dh), per_seq),
                pl.BlockSpec((None, lanes, dh), per_seq),
                pl.BlockSpec((None, 1, lanes), per_seq)]
    in_specs += [pl.BlockSpec((None, page, n_heads, dh), page_map(i, 3)) for i in range(npg)]
    in_specs += [pl.BlockSpec((None, page, n_heads, dh), page_map(i, 3)) for i in range(npg)]
    in_specs += [pl.BlockSpec((None, n_keys // lanes, lanes), page_map(i, 2)) for i in range(npg)]
    grid_spec = pltpu.PrefetchScalarGridSpec(
        num_scalar_prefetch=1,
        grid=(bs, n_steps),
        in_specs=in_specs,
        out_specs=pl.BlockSpec((None, rows, dh), per_seq),
        scratch_shapes=[pltpu.VMEM((rows, n_keys), F32), pltpu.VMEM((rows, 1), F32),
                        pltpu.VMEM((rows, 1), F32), pltpu.VMEM((rows, dh), F32),
                        pltpu.VMEM((8, lanes), F32), pltpu.VMEM((rows, 1), F32)],
    )
    return pl.pallas_call(
        functools.partial(_attn_sample_body, n_heads=n_heads, ts=ts, scale=dh ** -0.5),
        out_shape=jax.ShapeDtypeStruct((bs, rows, dh), F32),
        grid_spec=grid_spec,
        compiler_params=_params(("parallel", "arbitrary")),
        name="attn_sample",
    )(page_table, q2, kn2, vn2, lfn, *([cache_k] * npg), *([cache_v] * npg), *([lfc] * npg))


def kernel(x_prompt, x_sample, state_conv, cache_k, cache_v, cache_logf, page_table, norm_gain,
           ffn1_w_gate, ffn1_w_up, ffn1_w_down, ffn2_w_gate, ffn2_w_up, ffn2_w_down,
           conv_w_pw1, conv_b_pw1, conv_w_dw, conv_b_dw, conv_ln_g, conv_ln_b, conv_w_pw2,
           conv_b_pw2, kv_norm_g, w_k, w_v, w_fgate, b_fgate, attn_w_q, attn_w_o):
    n_seq, seq_len, d = x_prompt.shape
    bs, ts, _ = x_sample.shape
    depth = norm_gain.shape[0]
    n_heads = w_fgate.shape[-1]
    dh = w_k.shape[-1] // n_heads
    hist = conv_w_dw.shape[1] - 1
    assert depth == 2 and conv_w_pw1.shape[0] == 1 and attn_w_q.shape[0] == 1
    assert seq_len >= hist and ts * n_heads == 128
    mp, ms = n_seq * seq_len, bs * ts
    m = mp + ms
    tm = m // 8
    assert m % 8 == 0 and tm % 16 == 0

    x = jnp.concatenate([x_prompt.reshape(mp, d), x_sample.reshape(ms, d)], axis=0)
    gains = norm_gain.reshape(depth * N_NORMS, 1, d)
    ffn = functools.partial(_ffn, tm=tm, tf=256)

    h = ffn(x, gains, 0, 1, ffn1_w_gate, ffn1_w_up, ffn1_w_down, 0)
    u = _glu_proj(h, gains, 2, conv_w_pw1, conv_b_pw1, tm=tm, tn=512)
    u_s = u[mp:].reshape(bs, ts, d)
    y_p = _conv_prompt(u, n_seq, seq_len, conv_w_dw, conv_b_dw, conv_ln_g, conv_ln_b, tt=256)
    y_s, new_conv_s = _conv_sample(u_s, state_conv, conv_w_dw, conv_b_dw, conv_ln_g, conv_ln_b)
    y = jnp.concatenate([y_p, y_s.reshape(ms, d).astype(BF16)], axis=0)
    h = _resid_mm(h, y, conv_w_pw2, conv_b_pw2[0], gains, 3, tm=tm, tk=512)
    h = ffn(h, gains, 4, 5, ffn2_w_gate, ffn2_w_up, ffn2_w_down, 0)
    new_conv_p = u[:mp].reshape(n_seq, seq_len, d)[None, :, seq_len - hist:, :]

    k, v, logf = _kv_proj(h, kv_norm_g, w_k, w_v, w_fgate, b_fgate, tm=tm, tn=512)
    k_s = k[mp:].reshape(bs, ts, n_heads, dh)
    v_s = v[mp:].reshape(bs, ts, n_heads, dh)
    lf_s = logf[mp:].reshape(bs, ts, n_heads)

    h = ffn(h, gains, N_NORMS + 0, N_NORMS + 1, ffn1_w_gate, ffn1_w_up, ffn1_w_down, 1)
    q = _q_proj(h, gains, N_NORMS + 2, attn_w_q, tm=tm, tn=512)
    c, ct = _cumsum_prompt(logf, n_seq, seq_len, tc=256)
    o_p = _attn_prompt(q, k, v, c, ct, n_seq, seq_len, n_heads, tq=512)
    q2 = q[mp:].reshape(bs, ts, n_heads, dh).transpose(0, 2, 1, 3).reshape(bs, n_heads * ts, dh)
    o2 = _attn_sample(q2, k_s.reshape(bs, ts * n_heads, dh), v_s.reshape(bs, ts * n_heads, dh),
                      lf_s.reshape(bs, 1, ts * n_heads), cache_k, cache_v, cache_logf,
                      page_table, n_heads, ts)
    o_s = o2.reshape(bs, n_heads, ts, dh).transpose(0, 2, 1, 3).reshape(ms, d)
    o = jnp.concatenate([o_p, o_s.astype(BF16)], axis=0)
    h = _resid_mm(h, o, attn_w_o, jnp.zeros((d,), F32), gains, N_NORMS + 3, tm=tm, tk=512)
    h = ffn(h, gains, N_NORMS + 4, N_NORMS + 5, ffn2_w_gate, ffn2_w_up, ffn2_w_down, 1)

    return (h[:mp].reshape(n_seq, seq_len, d), h[mp:].reshape(bs, ts, d),
            new_conv_p,
            k[:mp].reshape(n_seq, seq_len, n_heads, dh), v[:mp].reshape(n_seq, seq_len, n_heads, dh),
            logf[:mp].reshape(n_seq, seq_len, n_heads),
            new_conv_s, k_s, v_s, lf_s)
```

```python
import functools
import math

import jax
import jax.numpy as jnp
from jax import lax
from jax.experimental import pallas as pl
from jax.experimental.pallas import tpu as pltpu

F32 = jnp.float32
BF16 = jnp.bfloat16

RMS_EPS = 1e-6
LN_EPS = 1e-5
NEG_INF = -1e30
LOG2E = math.log2(math.e)
N_NORMS = 6
LANES = 128
SUBLANES = 8
PAGES_PER_STEP = 4

VMEM_LIMIT_BYTES = 56 * 1024 * 1024

TM_PROMPT = 1024
TF_FFN = 256
TN_PROJ = 512
TK_RESID = 512
TT_CONV = 256
TQ_ATTN = 512
TC_CUMSUM = 256

_NT = (((1,), (1,)), ((), ()))
_TN = (((0,), (0,)), ((), ()))


def _params(semantics):
    return pltpu.CompilerParams(dimension_semantics=semantics, vmem_limit_bytes=VMEM_LIMIT_BYTES)


def _rms(x, g):
    ms = jnp.mean(x * x, axis=-1, keepdims=True)
    return (x * lax.rsqrt(ms + RMS_EPS)) * g


def _dot(a, b):
    return jnp.dot(a, b, preferred_element_type=F32)


def _split3(x):
    hi = x.astype(BF16)
    r1 = x - hi.astype(F32)
    mid = r1.astype(BF16)
    lo = (r1 - mid.astype(F32)).astype(BF16)
    return hi, mid, lo


def _ones_where(mask):
    return jnp.where(mask, 1.0, 0.0).astype(BF16)


def _lane_tile(x, reps):
    return jnp.concatenate([x] * reps, axis=1)


def _ffn_body(x_ref, gpre_ref, gpost_ref, wg_ref, wu_ref, wd_ref, o_ref, xn_ref, *, d_ff, tf):
    f = pl.program_id(1)

    @pl.when(f == 0)
    def _():
        xn_ref[...] = _rms(x_ref[...], gpre_ref[...]).astype(BF16)
        o_ref[...] = jnp.zeros_like(o_ref)

    xn = xn_ref[...]
    gate = _dot(xn, wg_ref[...].astype(BF16))
    up = _dot(xn, wu_ref[...].astype(BF16))
    act = gate * jax.nn.sigmoid(gate) * up
    col = f * tf + lax.broadcasted_iota(jnp.int32, (1, tf), 1)
    act = jnp.where(col < d_ff, act, 0.0).astype(BF16)
    row = f * tf + lax.broadcasted_iota(jnp.int32, (tf, 1), 0)
    wd = jnp.where(row < d_ff, wd_ref[...], 0.0).astype(BF16)
    o_ref[...] += _dot(act, wd)

    @pl.when(f == pl.num_programs(1) - 1)
    def _():
        o_ref[...] = x_ref[...] + 0.5 * _rms(o_ref[...], gpost_ref[...])


def _ffn(h, gains, n_pre, n_post, w_gate, w_up, w_down, layer, *, tm):
    m, d = h.shape
    d_ff = w_gate.shape[-1]
    tf = TF_FFN
    one = pl.Buffered(1)
    return pl.pallas_call(
        functools.partial(_ffn_body, d_ff=d_ff, tf=tf),
        out_shape=jax.ShapeDtypeStruct((m, d), F32),
        grid=(m // tm, pl.cdiv(d_ff, tf)),
        in_specs=[
            pl.BlockSpec((tm, d), lambda i, f: (i, 0), pipeline_mode=one),
            pl.BlockSpec((None, 1, d), lambda i, f: (n_pre, 0, 0)),
            pl.BlockSpec((None, 1, d), lambda i, f: (n_post, 0, 0)),
            pl.BlockSpec((None, d, tf), lambda i, f: (layer, 0, f)),
            pl.BlockSpec((None, d, tf), lambda i, f: (layer, 0, f)),
            pl.BlockSpec((None, tf, d), lambda i, f: (layer, f, 0)),
        ],
        out_specs=pl.BlockSpec((tm, d), lambda i, f: (i, 0), pipeline_mode=one),
        scratch_shapes=[pltpu.VMEM((tm, d), BF16)],
        compiler_params=_params(("parallel", "arbitrary")),
        name="ffn",
    )(h, gains, gains, w_gate, w_up, w_down)


def _glu_body(x_ref, g_ref, wa_ref, wg_ref, ba_ref, bg_ref, o_ref, xn_ref):
    @pl.when(pl.program_id(1) == 0)
    def _():
        xn_ref[...] = _rms(x_ref[...], g_ref[...]).astype(BF16)

    xn = xn_ref[...]
    a = _dot(xn, wa_ref[...].astype(BF16)) + ba_ref[...]
    gate = _dot(xn, wg_ref[...].astype(BF16)) + bg_ref[...]
    o_ref[...] = a * jax.nn.sigmoid(gate)


def _glu_proj(h, gains, n_gain, w_pw1, b_pw1, *, tm):
    m, d = h.shape
    dc = w_pw1.shape[-1] // 2
    tn = TN_PROJ
    nj = dc // tn
    b3 = b_pw1.reshape(b_pw1.shape[0], 1, 2 * dc)
    return pl.pallas_call(
        _glu_body,
        out_shape=jax.ShapeDtypeStruct((m, dc), F32),
        grid=(m // tm, nj),
        in_specs=[
            pl.BlockSpec((tm, d), lambda i, j: (i, 0), pipeline_mode=pl.Buffered(1)),
            pl.BlockSpec((None, 1, d), lambda i, j: (n_gain, 0, 0)),
            pl.BlockSpec((None, d, tn), lambda i, j: (0, 0, j)),
            pl.BlockSpec((None, d, tn), lambda i, j: (0, 0, j + nj)),
            pl.BlockSpec((None, 1, tn), lambda i, j: (0, 0, j)),
            pl.BlockSpec((None, 1, tn), lambda i, j: (0, 0, j + nj)),
        ],
        out_specs=pl.BlockSpec((tm, tn), lambda i, j: (i, j)),
        scratch_shapes=[pltpu.VMEM((tm, d), BF16)],
        compiler_params=_params(("parallel", "arbitrary")),
        name="glu_proj",
    )(h, gains, w_pw1, w_pw1, b3, b3)


def _kv_body(x_ref, g_ref, wk_ref, wv_ref, wf_ref, bf_ref, k_ref, v_ref, lf_ref, *rest, blocked):
    xn_ref = rest[-1]

    @pl.when(pl.program_id(1) == 0)
    def _():
        xn = _rms(x_ref[...], g_ref[...]).astype(BF16)
        xn_ref[...] = xn
        z = _dot(xn, wf_ref[...].astype(BF16)) + bf_ref[...]
        lf_ref[...] = jnp.minimum(z, 0.0) - jnp.log1p(jnp.exp(-jnp.abs(z)))

    xn = xn_ref[...]
    k = _dot(xn, wk_ref[...].astype(BF16))
    v = _dot(xn, wv_ref[...].astype(BF16))
    k_ref[...] = k
    v_ref[...] = v
    if blocked:
        kb_ref, vb_ref = rest[0], rest[1]
        for c in range(kb_ref.shape[0]):
            cs = slice(c * LANES, (c + 1) * LANES)
            kb_ref[c] = k[:, cs].astype(BF16)
            vb_ref[c] = v[:, cs].astype(BF16)


def _kv_proj(h, g, w_k, w_v, w_f, b_f, *, tm, blocked):
    m, d = h.shape
    da = w_k.shape[-1]
    nh = w_f.shape[-1]
    tn = TN_PROJ
    nc = tn // LANES
    out_shape = [jax.ShapeDtypeStruct((m, da), F32), jax.ShapeDtypeStruct((m, da), F32),
                 jax.ShapeDtypeStruct((m, nh), F32)]
    out_specs = [pl.BlockSpec((tm, tn), lambda i, j: (i, j)),
                 pl.BlockSpec((tm, tn), lambda i, j: (i, j)),
                 pl.BlockSpec((tm, nh), lambda i, j: (i, 0))]
    if blocked:
        out_shape += [jax.ShapeDtypeStruct((da // LANES, m, LANES), BF16)] * 2
        out_specs += [pl.BlockSpec((nc, tm, LANES), lambda i, j: (j, i, 0))] * 2
    return pl.pallas_call(
        functools.partial(_kv_body, blocked=blocked),
        out_shape=tuple(out_shape),
        grid=(m // tm, da // tn),
        in_specs=[
            pl.BlockSpec((tm, d), lambda i, j: (i, 0), pipeline_mode=pl.Buffered(1)),
            pl.BlockSpec((1, d), lambda i, j: (0, 0)),
            pl.BlockSpec((d, tn), lambda i, j: (0, j)),
            pl.BlockSpec((d, tn), lambda i, j: (0, j)),
            pl.BlockSpec((d, nh), lambda i, j: (0, 0)),
            pl.BlockSpec((1, nh), lambda i, j: (0, 0)),
        ],
        out_specs=tuple(out_specs),
        scratch_shapes=[pltpu.VMEM((tm, d), BF16)],
        compiler_params=_params(("parallel", "arbitrary")),
        name="kv_proj",
    )(h, g.reshape(1, d), w_k, w_v, w_f, b_f.reshape(1, nh))


def _q_body(x_ref, g_ref, w_ref, o_ref, xn_ref):
    @pl.when(pl.program_id(1) == 0)
    def _():
        xn_ref[...] = _rms(x_ref[...], g_ref[...]).astype(BF16)

    q = _dot(xn_ref[...], w_ref[...].astype(BF16))
    for c in range(o_ref.shape[0]):
        o_ref[c] = q[:, c * LANES:(c + 1) * LANES].astype(o_ref.dtype)


def _q_proj(h, gains, n_gain, w_q, *, tm):
    m, d = h.shape
    da = w_q.shape[-1]
    tn = TN_PROJ
    nc = tn // LANES
    return pl.pallas_call(
        _q_body,
        out_shape=jax.ShapeDtypeStruct((da // LANES, m, LANES), BF16),
        grid=(m // tm, da // tn),
        in_specs=[
            pl.BlockSpec((tm, d), lambda i, j: (i, 0), pipeline_mode=pl.Buffered(1)),
            pl.BlockSpec((None, 1, d), lambda i, j: (n_gain, 0, 0)),
            pl.BlockSpec((None, d, tn), lambda i, j: (0, 0, j)),
        ],
        out_specs=pl.BlockSpec((nc, tm, LANES), lambda i, j: (j, i, 0)),
        scratch_shapes=[pltpu.VMEM((tm, d), BF16)],
        compiler_params=_params(("parallel", "arbitrary")),
        name="q_proj",
    )(h, gains, w_q)


def _resid_mm_body(h_ref, y_ref, w_ref, b_ref, g_ref, o_ref):
    k = pl.program_id(1)

    @pl.when(k == 0)
    def _():
        o_ref[...] = jnp.broadcast_to(b_ref[...], o_ref.shape)

    y = jnp.concatenate([y_ref[c] for c in range(y_ref.shape[0])], axis=1)
    o_ref[...] += _dot(y, w_ref[...].astype(BF16))

    @pl.when(k == pl.num_programs(1) - 1)
    def _():
        o_ref[...] = h_ref[...] + _rms(o_ref[...], g_ref[...])


def _resid_mm(h, y, w, b, gains, n_gain, *, tm):
    m, d = h.shape
    tk = TK_RESID
    nc = tk // LANES
    one = pl.Buffered(1)
    return pl.pallas_call(
        _resid_mm_body,
        out_shape=jax.ShapeDtypeStruct((m, d), F32),
        grid=(m // tm, y.shape[0] // nc),
        in_specs=[
            pl.BlockSpec((tm, d), lambda i, k: (i, 0), pipeline_mode=one),
            pl.BlockSpec((nc, tm, LANES), lambda i, k: (k, i, 0)),
            pl.BlockSpec((None, tk, d), lambda i, k: (0, k, 0)),
            pl.BlockSpec((1, d), lambda i, k: (0, 0)),
            pl.BlockSpec((None, 1, d), lambda i, k: (n_gain, 0, 0)),
        ],
        out_specs=pl.BlockSpec((tm, d), lambda i, k: (i, 0), pipeline_mode=one),
        compiler_params=_params(("parallel", "arbitrary")),
        name="resid_mm",
    )(h, y, w, b.reshape(1, d), gains)


def _ln_silu(y, g, b):
    mu = jnp.mean(y, axis=-1, keepdims=True)
    dlt = y - mu
    var = jnp.mean(dlt * dlt, axis=-1, keepdims=True)
    z = (dlt * lax.rsqrt(var + LN_EPS)) * g + b
    return z * jax.nn.sigmoid(z)


_CONV_COLS = 512
_HALO = 32


def _conv_prompt_body(halo_ref, u_ref, wdw_ref, bdw_ref, lng_ref, lnb_ref, o_ref, ubuf_ref, ybuf_ref,
                      *, tt, width):
    i = pl.program_id(1)
    d = u_ref.shape[-1]
    ubuf_ref[pl.ds(0, _HALO), :] = jnp.where(i > 0, halo_ref[...], 0.0)
    ubuf_ref[pl.ds(_HALO, tt), :] = u_ref[...]
    off = _HALO - (width - 1)
    n_tiles = (off + width - 1 + SUBLANES - 1) // SUBLANES + 1

    def conv_chunk(r, carry):
        base = pl.multiple_of(r * SUBLANES, SUBLANES)
        sub = lax.broadcasted_iota(jnp.int32, (SUBLANES, _CONV_COLS), 0)
        for cb in range(d // _CONV_COLS):
            cs = slice(cb * _CONV_COLS, (cb + 1) * _CONV_COLS)
            tiles = [ubuf_ref[pl.ds(base + SUBLANES * a, SUBLANES), cs] for a in range(n_tiles)]
            acc = jnp.broadcast_to(bdw_ref[:, cs], (SUBLANES, _CONV_COLS))
            for s in range(SUBLANES):
                z = None
                for a in range(n_tiles):
                    w = SUBLANES * a + s - off
                    if not 0 <= w < width:
                        continue
                    src = tiles[a] if s == 0 else jnp.where(sub >= s, tiles[a], tiles[a + 1])
                    term = src * wdw_ref[w:w + 1, cs]
                    z = term if z is None else z + term
                acc = acc + (z if s == 0 else pltpu.roll(z, SUBLANES - s, 0))
            ybuf_ref[pl.ds(base, SUBLANES), cs] = acc
        return carry

    lax.fori_loop(0, tt // SUBLANES, conv_chunk, 0)

    def ln_chunk(r, carry):
        base = pl.multiple_of(r * 16, 16)
        y = _ln_silu(ybuf_ref[pl.ds(base, 16), :], lng_ref[...], lnb_ref[...]).astype(o_ref.dtype)
        for c in range(o_ref.shape[0]):
            o_ref[c, pl.ds(base, 16), :] = y[:, c * LANES:(c + 1) * LANES]
        return carry

    lax.fori_loop(0, tt // 16, ln_chunk, 0)


def _conv_prompt(u, n_seq, seq_len, w_dw, b_dw, ln_g, ln_b):
    d = u.shape[-1]
    width = w_dw.shape[1]
    tt = TT_CONV
    nt = seq_len // tt
    hb = tt // _HALO
    return pl.pallas_call(
        functools.partial(_conv_prompt_body, tt=tt, width=width),
        out_shape=jax.ShapeDtypeStruct((d // LANES, n_seq * seq_len, LANES), BF16),
        grid=(n_seq, nt),
        in_specs=[
            pl.BlockSpec((_HALO, d), lambda b, i: (jnp.maximum((b * nt + i) * hb - 1, 0), 0)),
            pl.BlockSpec((tt, d), lambda b, i: (b * nt + i, 0)),
            pl.BlockSpec((None, width, d), lambda b, i: (0, 0, 0)),
            pl.BlockSpec((1, d), lambda b, i: (0, 0)),
            pl.BlockSpec((1, d), lambda b, i: (0, 0)),
            pl.BlockSpec((1, d), lambda b, i: (0, 0)),
        ],
        out_specs=pl.BlockSpec((d // LANES, tt, LANES), lambda b, i: (0, b * nt + i, 0)),
        scratch_shapes=[pltpu.VMEM((tt + _HALO, d), F32), pltpu.VMEM((tt, d), F32)],
        compiler_params=_params(("parallel", "parallel")),
        name="conv_prompt",
    )(u, u, w_dw, b_dw, ln_g, ln_b)


def _conv_sample_body(st_ref, u_ref, wdw_ref, bdw_ref, lng_ref, lnb_ref, y_ref, ns_ref, ubuf_ref,
                      *, ts, width):
    hist = width - 1
    ubuf_ref[pl.ds(0, hist), :] = st_ref[...]
    ubuf_ref[pl.ds(hist, ts), :] = u_ref[...]
    full = ubuf_ref[...]
    acc = jnp.broadcast_to(bdw_ref[...], (ts, full.shape[-1]))
    for w in range(width):
        acc = acc + full[w:w + ts] * wdw_ref[w:w + 1, :]
    y_ref[...] = _ln_silu(acc, lng_ref[...], lnb_ref[...])
    ns_ref[...] = full[ts:ts + hist]


def _conv_sample(u_s, state, w_dw, b_dw, ln_g, ln_b):
    bs, ts, d = u_s.shape
    width = w_dw.shape[1]
    hist = width - 1
    return pl.pallas_call(
        functools.partial(_conv_sample_body, ts=ts, width=width),
        out_shape=(jax.ShapeDtypeStruct((bs, ts, d), F32),
                   jax.ShapeDtypeStruct((1, bs, hist, d), F32)),
        grid=(bs,),
        in_specs=[
            pl.BlockSpec((None, None, hist, d), lambda b: (0, b, 0, 0)),
            pl.BlockSpec((None, ts, d), lambda b: (b, 0, 0)),
            pl.BlockSpec((None, width, d), lambda b: (0, 0, 0)),
            pl.BlockSpec((1, d), lambda b: (0, 0)),
            pl.BlockSpec((1, d), lambda b: (0, 0)),
            pl.BlockSpec((1, d), lambda b: (0, 0)),
        ],
        out_specs=(pl.BlockSpec((None, ts, d), lambda b: (b, 0, 0)),
                   pl.BlockSpec((None, None, hist, d), lambda b: (0, b, 0, 0))),
        scratch_shapes=[pltpu.VMEM((hist + ts, d), F32)],
        compiler_params=_params(("parallel",)),
        name="conv_sample",
    )(state, u_s, w_dw, b_dw, ln_g, ln_b)


def _cumsum_body(x_ref, ct_ref, carry_ref, *, tc):
    @pl.when(pl.program_id(1) == 0)
    def _():
        carry_ref[...] = jnp.zeros_like(carry_ref)

    row = lax.broadcasted_iota(jnp.int32, (tc, tc), 0)
    col = lax.broadcasted_iota(jnp.int32, (tc, tc), 1)
    upper = _ones_where(row <= col)
    ctb = carry_ref[...]
    for piece in _split3(x_ref[...]):
        ctb = ctb + lax.dot_general(piece, upper, _TN, preferred_element_type=F32)
    for h in range(ct_ref.shape[0]):
        ct_ref[h] = ctb[h:h + 1, :]
    carry_ref[...] = ctb[:, tc - 1:tc]


def _cumsum_prompt(logf, n_seq, seq_len):
    nh = logf.shape[-1]
    tc = TC_CUMSUM
    nt = seq_len // tc
    return pl.pallas_call(
        functools.partial(_cumsum_body, tc=tc),
        out_shape=jax.ShapeDtypeStruct((n_seq, nh, 1, seq_len), F32),
        grid=(n_seq, nt),
        in_specs=[pl.BlockSpec((tc, nh), lambda b, j: (b * nt + j, 0))],
        out_specs=pl.BlockSpec((None, nh, 1, tc), lambda b, j: (b, 0, 0, j)),
        scratch_shapes=[pltpu.VMEM((nh, 1), F32)],
        compiler_params=_params(("parallel", "arbitrary")),
        name="cumsum_prompt",
    )(logf)


def _attn_prompt_body(q_ref, k_ref, v_ref, ctq_ref, ctk_ref, o_ref, m_ref, l_ref, acc_ref, cq_ref,
                      *, scale2):
    qi = pl.program_id(1)
    ki = pl.program_id(2)
    n_heads, tq, dh = q_ref.shape
    tk = k_ref.shape[1]
    reps = tk // dh

    @pl.when(ki == 0)
    def _():
        m_ref[...] = jnp.full_like(m_ref, NEG_INF)
        l_ref[...] = jnp.zeros_like(l_ref)
        acc_ref[...] = jnp.zeros_like(acc_ref)

        def init_head(h, carry):
            row = ctq_ref[h] * LOG2E
            cq_ref[h] = jnp.broadcast_to(row, (dh, tq)).T
            return carry

        lax.fori_loop(0, n_heads, init_head, 0)

    def run(diagonal):
        def head(h, carry):
            s = lax.dot_general(q_ref[h], k_ref[h], _NT, preferred_element_type=F32) * scale2
            s = s + _lane_tile(cq_ref[h], reps) - ctk_ref[h] * LOG2E
            if diagonal:
                row = lax.broadcasted_iota(jnp.int32, (tq, tk), 0)
                col = lax.broadcasted_iota(jnp.int32, (tq, tk), 1)
                s = jnp.where(col <= row, s, NEG_INF)
            m_prev = m_ref[h]
            m_new = jnp.maximum(m_prev, jnp.max(s, axis=-1, keepdims=True))
            alpha = jnp.exp2(m_prev - m_new)
            p = jnp.exp2(s - _lane_tile(m_new, reps))
            l_ref[h] = alpha * l_ref[h] + jnp.sum(p, axis=-1, keepdims=True)
            acc_ref[h] = alpha * acc_ref[h] + _dot(p.astype(BF16), v_ref[h])
            m_ref[h] = m_new
            return carry

        lax.fori_loop(0, n_heads, head, 0)

    @pl.when(ki < qi)
    def _():
        run(False)

    @pl.when(ki == qi)
    def _():
        run(True)

        def finish(h, carry):
            o_ref[h] = (acc_ref[h] / l_ref[h]).astype(o_ref.dtype)
            return carry

        lax.fori_loop(0, n_heads, finish, 0)


def _attn_prompt(q, k, v, ct, n_seq, seq_len):
    n_heads, _, dh = q.shape
    tq = TQ_ATTN
    nq = seq_len // tq
    stat = pltpu.VMEM((n_heads, tq, dh), F32)
    kv_map = lambda b, qi, ki: (0, b * nq + jnp.minimum(ki, qi), 0)
    return pl.pallas_call(
        functools.partial(_attn_prompt_body, scale2=dh ** -0.5 * LOG2E),
        out_shape=jax.ShapeDtypeStruct((n_heads, n_seq * seq_len, dh), BF16),
        grid=(n_seq, nq, nq),
        in_specs=[
            pl.BlockSpec((n_heads, tq, dh), lambda b, qi, ki: (0, b * nq + qi, 0)),
            pl.BlockSpec((n_heads, tq, dh), kv_map),
            pl.BlockSpec((n_heads, tq, dh), kv_map),
            pl.BlockSpec((None, n_heads, 1, tq), lambda b, qi, ki: (b, 0, 0, qi)),
            pl.BlockSpec((None, n_heads, 1, tq), lambda b, qi, ki: (b, 0, 0, jnp.minimum(ki, qi))),
        ],
        out_specs=pl.BlockSpec((n_heads, tq, dh), lambda b, qi, ki: (0, b * nq + qi, 0)),
        scratch_shapes=[stat, stat, stat, stat],
        compiler_params=_params(("parallel", "parallel", "arbitrary")),
        name="attn_prompt",
    )(q, k, v, ct, ct)


def _lane_iota(shape):
    return lax.broadcasted_iota(jnp.int32, shape, len(shape) - 1)


def _suffix_sums_page(x, n_heads):
    n_rows, n_lanes = x.shape
    lane = _lane_iota(x.shape)
    sub = lax.broadcasted_iota(jnp.int32, x.shape, 0)
    y = x
    sh = n_heads
    while sh < n_lanes:
        y = y + jnp.where(lane + sh < n_lanes, pltpu.roll(y, n_lanes - sh, 1), 0.0)
        sh *= 2
    z = jnp.where(lane < n_heads, y, 0.0)
    sh = n_heads
    while sh < n_lanes:
        z = z + pltpu.roll(z, sh, 1)
        sh *= 2
    zi = z
    sh = 1
    while sh < n_rows:
        zi = zi + jnp.where(sub + sh < n_rows, pltpu.roll(zi, n_rows - sh, 0), 0.0)
        sh *= 2
    return y + (zi - z), zi[0:1, :]


def _attn_sample_body(pt_ref, q_ref, kn_ref, vn_ref, lfn_ref, *refs, n_heads, ts, scale):
    npg = PAGES_PER_STEP
    k_refs = refs[0:npg]
    v_refs = refs[npg:2 * npg]
    lf_refs = refs[2 * npg:3 * npg]
    o_ref, mask_ref, m_ref, l_ref, acc_ref, tail_ref, cq_ref = refs[3 * npg:]
    del pt_ref
    j = pl.program_id(1)
    rows, dh = q_ref.shape
    page = k_refs[0].shape[0]
    n_keys = page * n_heads

    def online_update(s, v_bf16, first):
        if first:
            m_new = jnp.max(s, axis=-1, keepdims=True)
            p = jnp.exp(s - m_new)
            l_ref[...] = jnp.sum(p, axis=-1, keepdims=True)
            acc_ref[...] = _dot(p.astype(BF16), v_bf16)
        else:
            m_prev = m_ref[...]
            m_new = jnp.maximum(m_prev, jnp.max(s, axis=-1, keepdims=True))
            alpha = jnp.exp(m_prev - m_new)
            p = jnp.exp(s - m_new)
            l_ref[...] = alpha * l_ref[...] + jnp.sum(p, axis=-1, keepdims=True)
            acc_ref[...] = alpha * acc_ref[...] + _dot(p.astype(BF16), v_bf16)
        m_ref[...] = m_new

    @pl.when(j == 0)
    def _():
        r_i = lax.broadcasted_iota(jnp.int32, (rows, n_keys), 0)
        l_i = _lane_iota((rows, n_keys))
        mask_ref[...] = jnp.where(l_i % n_heads == r_i // ts, 0.0, NEG_INF)

        n_new = ts * n_heads
        cn = jnp.broadcast_to(lfn_ref[...], (8, n_new))
        lane8 = _lane_iota((8, n_new))
        sh = n_heads
        while sh < n_new:
            cn = cn + jnp.where(lane8 >= sh, pltpu.roll(cn, sh, 1), 0.0)
            sh *= 2
        cn_row = cn[0:1, :]
        r2 = lax.broadcasted_iota(jnp.int32, (rows, n_new), 0)
        l2 = _lane_iota((rows, n_new))
        own = l2 == (r2 % ts) * n_heads + r2 // ts
        cq = jnp.sum(jnp.where(own, jnp.broadcast_to(cn_row, (rows, n_new)), 0.0),
                     axis=-1, keepdims=True)
        cq_ref[...] = cq
        tail_ref[...] = jnp.zeros_like(tail_ref)

        s = lax.dot_general(q_ref[...], kn_ref[...].astype(BF16), _NT,
                            preferred_element_type=F32) * scale
        s = s + cq - cn_row
        valid = (l2 % n_heads == r2 // ts) & (l2 // n_heads <= r2 % ts)
        s = jnp.where(valid, s, NEG_INF)
        online_update(s, vn_ref[...].astype(BF16), first=True)

    tail = tail_ref[0:1, :]
    e_tiles = [None] * npg
    for i in reversed(range(npg)):
        x = lf_refs[i][...]
        incl, tot = _suffix_sums_page(x, n_heads)
        e_tiles[i] = incl - x + tail
        tail = tail + tot
    tail_ref[...] = jnp.broadcast_to(tail, tail_ref.shape)
    e_row = jnp.concatenate([e[a:a + 1, :] for e in e_tiles for a in range(e.shape[0])], axis=1)

    k4 = jnp.concatenate([r[...].reshape(n_keys, dh).astype(BF16) for r in k_refs], axis=0)
    v4 = jnp.concatenate([r[...].reshape(n_keys, dh).astype(BF16) for r in v_refs], axis=0)
    s = lax.dot_general(q_ref[...], k4, _NT, preferred_element_type=F32) * scale
    s = s + cq_ref[...] + e_row
    s = s + jnp.concatenate([mask_ref[...]] * npg, axis=1)
    online_update(s, v4, first=False)

    @pl.when(j == pl.num_programs(1) - 1)
    def _():
        o_ref[...] = acc_ref[...] / l_ref[...]


def _attn_sample(q2, kn2, vn2, lfn, cache_k, cache_v, cache_logf, page_table, n_heads, ts):
    bs, rows, dh = q2.shape
    n_phys, page = cache_k.shape[0], cache_k.shape[1]
    n_pages = page_table.shape[1]
    npg = PAGES_PER_STEP
    n_steps = n_pages // npg
    n_keys = page * n_heads
    lanes = kn2.shape[1]
    lfc = cache_logf.reshape(n_phys, n_keys // lanes, lanes)

    def page_map(i, nd):
        return lambda b, j, pt: (pt[b, (n_steps - 1 - j) * npg + i],) + (0,) * nd

    per_seq = lambda b, j, pt: (b, 0, 0)
    in_specs = [pl.BlockSpec((None, rows, dh), per_seq),
                pl.BlockSpec((None, lanes, dh), per_seq),
                pl.BlockSpec((None, lanes, dh), per_seq),
                pl.BlockSpec((None, 1, lanes), per_seq)]
    in_specs += [pl.BlockSpec((None, page, n_heads, dh), page_map(i, 3)) for i in range(npg)]
    in_specs += [pl.BlockSpec((None, page, n_heads, dh), page_map(i, 3)) for i in range(npg)]
    in_specs += [pl.BlockSpec((None, n_keys // lanes, lanes), page_map(i, 2)) for i in range(npg)]
    grid_spec = pltpu.PrefetchScalarGridSpec(
        num_scalar_prefetch=1,
        grid=(bs, n_steps),
        in_specs=in_specs,
        out_specs=pl.BlockSpec((None, rows, dh), per_seq),
        scratch_shapes=[pltpu.VMEM((rows, n_keys), F32), pltpu.VMEM((rows, 1), F32),
                        pltpu.VMEM((rows, 1), F32), pltpu.VMEM((rows, dh), F32),
                        pltpu.VMEM((8, lanes), F32), pltpu.VMEM((rows, 1), F32)],
    )
    return pl.pallas_call(
        functools.partial(_attn_sample_body, n_heads=n_heads, ts=ts, scale=dh ** -0.5),
        out_shape=jax.ShapeDtypeStruct((bs, rows, dh), F32),
        grid_spec=grid_spec,
        compiler_params=_params(("parallel", "arbitrary")),
        name="attn_sample",
    )(page_table, q2, kn2, vn2, lfn, *([cache_k] * npg), *([cache_v] * npg), *([lfc] * npg))


def _to_blocks(x):
    rows, c = x.shape
    return x.reshape(rows, c // LANES, LANES).transpose(1, 0, 2)


def kernel(x_prompt, x_sample, state_conv, cache_k, cache_v, cache_logf, page_table, norm_gain,
           ffn1_w_gate, ffn1_w_up, ffn1_w_down, ffn2_w_gate, ffn2_w_up, ffn2_w_down,
           conv_w_pw1, conv_b_pw1, conv_w_dw, conv_b_dw, conv_ln_g, conv_ln_b, conv_w_pw2,
           conv_b_pw2, kv_norm_g, w_k, w_v, w_fgate, b_fgate, attn_w_q, attn_w_o):
    n_seq, seq_len, d = x_prompt.shape
    bs, ts, _ = x_sample.shape
    depth = norm_gain.shape[0]
    n_heads = w_fgate.shape[-1]
    dh = w_k.shape[-1] // n_heads
    hist = conv_w_dw.shape[1] - 1
    assert depth == 2 and conv_w_pw1.shape[0] == 1 and attn_w_q.shape[0] == 1
    assert seq_len >= hist and ts * n_heads == LANES and dh == LANES
    mp, ms = n_seq * seq_len, bs * ts
    gains = norm_gain.reshape(depth * N_NORMS, 1, d)
    zero_bias = jnp.zeros((d,), F32)

    def trunk_a(h, tm):
        h = _ffn(h, gains, 0, 1, ffn1_w_gate, ffn1_w_up, ffn1_w_down, 0, tm=tm)
        return h, _glu_proj(h, gains, 2, conv_w_pw1, conv_b_pw1, tm=tm)

    def trunk_b(h, y, tm, blocked):
        h = _resid_mm(h, y, conv_w_pw2, conv_b_pw2[0], gains, 3, tm=tm)
        h = _ffn(h, gains, 4, 5, ffn2_w_gate, ffn2_w_up, ffn2_w_down, 0, tm=tm)
        kv = _kv_proj(h, kv_norm_g, w_k, w_v, w_fgate, b_fgate, tm=tm, blocked=blocked)
        h = _ffn(h, gains, N_NORMS + 0, N_NORMS + 1, ffn1_w_gate, ffn1_w_up, ffn1_w_down, 1, tm=tm)
        return h, kv, _q_proj(h, gains, N_NORMS + 2, attn_w_q, tm=tm)

    def trunk_c(h, o, tm):
        h = _resid_mm(h, o, attn_w_o, zero_bias, gains, N_NORMS + 3, tm=tm)
        return _ffn(h, gains, N_NORMS + 4, N_NORMS + 5, ffn2_w_gate, ffn2_w_up, ffn2_w_down, 1,
                    tm=tm)

    hp, u_p = trunk_a(x_prompt.reshape(mp, d), TM_PROMPT)
    y_p = _conv_prompt(u_p, n_seq, seq_len, conv_w_dw, conv_b_dw, conv_ln_g, conv_ln_b)
    hp, (k_p, v_p, lf_p, kb_p, vb_p), q_p = trunk_b(hp, y_p, TM_PROMPT, True)
    ct = _cumsum_prompt(lf_p, n_seq, seq_len)
    o_p = _attn_prompt(q_p, kb_p, vb_p, ct, n_seq, seq_len)
    hp = trunk_c(hp, o_p, TM_PROMPT)
    new_conv_p = u_p.reshape(n_seq, seq_len, d)[None, :, seq_len - hist:, :]

    hs, u_s = trunk_a(x_sample.reshape(ms, d), ms)
    y_s, new_conv_s = _conv_sample(u_s.reshape(bs, ts, d), state_conv, conv_w_dw, conv_b_dw,
                                   conv_ln_g, conv_ln_b)
    hs, (k_s, v_s, lf_s), q_s = trunk_b(hs, _to_blocks(y_s.reshape(ms, d)).astype(BF16), ms, False)
    q2 = q_s.reshape(n_heads, bs, ts, dh).transpose(1, 0, 2, 3).reshape(bs, n_heads * ts, dh)
    o2 = _attn_sample(q2, k_s.reshape(bs, ts * n_heads, dh), v_s.reshape(bs, ts * n_heads, dh),
                      lf_s.reshape(bs, 1, ts * n_heads), cache_k, cache_v, cache_logf,
                      page_table, n_heads, ts)
    o_s = o2.reshape(bs, n_heads, ts, dh).transpose(1, 0, 2, 3).reshape(n_heads, ms, dh)
    hs = trunk_c(hs, o_s.astype(BF16), ms)

    return (hp.reshape(n_seq, seq_len, d), hs.reshape(bs, ts, d),
            new_conv_p,
            k_p.reshape(n_seq, seq_len, n_heads, dh), v_p.reshape(n_seq, seq_len, n_heads, dh),
            lf_p.reshape(n_seq, seq_len, n_heads),
            new_conv_s,
            k_s.reshape(bs, ts, n_heads, dh), v_s.reshape(bs, ts, n_heads, dh),
            lf_s.reshape(bs, ts, n_heads))
```

```python
import functools
import math

import jax
import jax.numpy as jnp
from jax import lax
from jax.experimental import pallas as pl
from jax.experimental.pallas import tpu as pltpu

F32 = jnp.float32
BF16 = jnp.bfloat16

RMS_EPS = 1e-6
LN_EPS = 1e-5
NEG_INF = -1e30
LOG2E = math.log2(math.e)
N_NORMS = 6
LANES = 128
SUBLANES = 8
PAGES_PER_STEP = 8

VMEM_LIMIT_BYTES = 56 * 1024 * 1024

TM_PROMPT = 1024
TF_FFN = 256
TN_PROJ = 512
TK_RESID = 512
TT_CONV = 256
TQ_ATTN = 512
TC_CUMSUM = 256

_NT = (((1,), (1,)), ((), ()))
_TN = (((0,), (0,)), ((), ()))


def _params(semantics):
    return pltpu.CompilerParams(dimension_semantics=semantics, vmem_limit_bytes=VMEM_LIMIT_BYTES)


def _rms(x, g):
    ms = jnp.mean(x * x, axis=-1, keepdims=True)
    return (x * lax.rsqrt(ms + RMS_EPS)) * g


def _dot(a, b):
    return jnp.dot(a, b, preferred_element_type=F32)


def _split3(x):
    hi = x.astype(BF16)
    r1 = x - hi.astype(F32)
    mid = r1.astype(BF16)
    lo = (r1 - mid.astype(F32)).astype(BF16)
    return hi, mid, lo


def _ones_where(mask):
    return jnp.where(mask, 1.0, 0.0).astype(BF16)


def _lane_tile(x, reps):
    return jnp.concatenate([x] * reps, axis=1)


def _ffn_body(x_ref, gpre_ref, gpost_ref, wg_ref, wu_ref, wd_ref, o_ref, xn_ref, *, d_ff, tf):
    f = pl.program_id(1)

    @pl.when(f == 0)
    def _():
        xn_ref[...] = _rms(x_ref[...], gpre_ref[...]).astype(BF16)
        o_ref[...] = jnp.zeros_like(o_ref)

    xn = xn_ref[...]
    gate = _dot(xn, wg_ref[...].astype(BF16))
    up = _dot(xn, wu_ref[...].astype(BF16))
    act = gate * jax.nn.sigmoid(gate) * up
    col = f * tf + lax.broadcasted_iota(jnp.int32, (1, tf), 1)
    act = jnp.where(col < d_ff, act, 0.0).astype(BF16)
    row = f * tf + lax.broadcasted_iota(jnp.int32, (tf, 1), 0)
    wd = jnp.where(row < d_ff, wd_ref[...], 0.0).astype(BF16)
    o_ref[...] += _dot(act, wd)

    @pl.when(f == pl.num_programs(1) - 1)
    def _():
        o_ref[...] = x_ref[...] + 0.5 * _rms(o_ref[...], gpost_ref[...])


def _ffn(h, gains, n_pre, n_post, w_gate, w_up, w_down, layer, *, tm):
    m, d = h.shape
    d_ff = w_gate.shape[-1]
    tf = TF_FFN
    one = pl.Buffered(1)
    return pl.pallas_call(
        functools.partial(_ffn_body, d_ff=d_ff, tf=tf),
        out_shape=jax.ShapeDtypeStruct((m, d), F32),
        grid=(m // tm, pl.cdiv(d_ff, tf)),
        in_specs=[
            pl.BlockSpec((tm, d), lambda i, f: (i, 0), pipeline_mode=one),
            pl.BlockSpec((None, 1, d), lambda i, f: (n_pre, 0, 0)),
            pl.BlockSpec((None, 1, d), lambda i, f: (n_post, 0, 0)),
            pl.BlockSpec((None, d, tf), lambda i, f: (layer, 0, f)),
            pl.BlockSpec((None, d, tf), lambda i, f: (layer, 0, f)),
            pl.BlockSpec((None, tf, d), lambda i, f: (layer, f, 0)),
        ],
        out_specs=pl.BlockSpec((tm, d), lambda i, f: (i, 0), pipeline_mode=one),
        scratch_shapes=[pltpu.VMEM((tm, d), BF16)],
        compiler_params=_params(("parallel", "arbitrary")),
        name="ffn",
    )(h, gains, gains, w_gate, w_up, w_down)


def _glu_body(x_ref, g_ref, wa_ref, wg_ref, ba_ref, bg_ref, o_ref, xn_ref):
    @pl.when(pl.program_id(1) == 0)
    def _():
        xn_ref[...] = _rms(x_ref[...], g_ref[...]).astype(BF16)

    xn = xn_ref[...]
    a = _dot(xn, wa_ref[...].astype(BF16)) + ba_ref[...]
    gate = _dot(xn, wg_ref[...].astype(BF16)) + bg_ref[...]
    o_ref[...] = a * jax.nn.sigmoid(gate)


def _glu_proj(h, gains, n_gain, w_pw1, b_pw1, *, tm):
    m, d = h.shape
    dc = w_pw1.shape[-1] // 2
    tn = TN_PROJ
    nj = dc // tn
    b3 = b_pw1.reshape(b_pw1.shape[0], 1, 2 * dc)
    return pl.pallas_call(
        _glu_body,
        out_shape=jax.ShapeDtypeStruct((m, dc), F32),
        grid=(m // tm, nj),
        in_specs=[
            pl.BlockSpec((tm, d), lambda i, j: (i, 0), pipeline_mode=pl.Buffered(1)),
            pl.BlockSpec((None, 1, d), lambda i, j: (n_gain, 0, 0)),
            pl.BlockSpec((None, d, tn), lambda i, j: (0, 0, j)),
            pl.BlockSpec((None, d, tn), lambda i, j: (0, 0, j + nj)),
            pl.BlockSpec((None, 1, tn), lambda i, j: (0, 0, j)),
            pl.BlockSpec((None, 1, tn), lambda i, j: (0, 0, j + nj)),
        ],
        out_specs=pl.BlockSpec((tm, tn), lambda i, j: (i, j)),
        scratch_shapes=[pltpu.VMEM((tm, d), BF16)],
        compiler_params=_params(("parallel", "arbitrary")),
        name="glu_proj",
    )(h, gains, w_pw1, w_pw1, b3, b3)


def _kv_body(x_ref, g_ref, wk_ref, wv_ref, wf_ref, bf_ref, k_ref, v_ref, lf_ref, *rest, blocked):
    xn_ref = rest[-1]

    @pl.when(pl.program_id(1) == 0)
    def _():
        xn = _rms(x_ref[...], g_ref[...]).astype(BF16)
        xn_ref[...] = xn
        z = _dot(xn, wf_ref[...].astype(BF16)) + bf_ref[...]
        lf_ref[...] = jnp.minimum(z, 0.0) - jnp.log1p(jnp.exp(-jnp.abs(z)))

    xn = xn_ref[...]
    k = _dot(xn, wk_ref[...].astype(BF16))
    v = _dot(xn, wv_ref[...].astype(BF16))
    k_ref[...] = k
    v_ref[...] = v
    if blocked:
        kb_ref, vb_ref = rest[0], rest[1]
        for c in range(kb_ref.shape[0]):
            cs = slice(c * LANES, (c + 1) * LANES)
            kb_ref[c] = k[:, cs].astype(BF16)
            vb_ref[c] = v[:, cs].astype(BF16)


def _kv_proj(h, g, w_k, w_v, w_f, b_f, *, tm, blocked):
    m, d = h.shape
    da = w_k.shape[-1]
    nh = w_f.shape[-1]
    tn = TN_PROJ
    nc = tn // LANES
    out_shape = [jax.ShapeDtypeStruct((m, da), F32), jax.ShapeDtypeStruct((m, da), F32),
                 jax.ShapeDtypeStruct((m, nh), F32)]
    out_specs = [pl.BlockSpec((tm, tn), lambda i, j: (i, j)),
                 pl.BlockSpec((tm, tn), lambda i, j: (i, j)),
                 pl.BlockSpec((tm, nh), lambda i, j: (i, 0))]
    if blocked:
        out_shape += [jax.ShapeDtypeStruct((da // LANES, m, LANES), BF16)] * 2
        out_specs += [pl.BlockSpec((nc, tm, LANES), lambda i, j: (j, i, 0))] * 2
    return pl.pallas_call(
        functools.partial(_kv_body, blocked=blocked),
        out_shape=tuple(out_shape),
        grid=(m // tm, da // tn),
        in_specs=[
            pl.BlockSpec((tm, d), lambda i, j: (i, 0), pipeline_mode=pl.Buffered(1)),
            pl.BlockSpec((1, d), lambda i, j: (0, 0)),
            pl.BlockSpec((d, tn), lambda i, j: (0, j)),
            pl.BlockSpec((d, tn), lambda i, j: (0, j)),
            pl.BlockSpec((d, nh), lambda i, j: (0, 0)),
            pl.BlockSpec((1, nh), lambda i, j: (0, 0)),
        ],
        out_specs=tuple(out_specs),
        scratch_shapes=[pltpu.VMEM((tm, d), BF16)],
        compiler_params=_params(("parallel", "arbitrary")),
        name="kv_proj",
    )(h, g.reshape(1, d), w_k, w_v, w_f, b_f.reshape(1, nh))


def _q_body(x_ref, g_ref, w_ref, o_ref, xn_ref):
    @pl.when(pl.program_id(1) == 0)
    def _():
        xn_ref[...] = _rms(x_ref[...], g_ref[...]).astype(BF16)

    q = _dot(xn_ref[...], w_ref[...].astype(BF16))
    for c in range(o_ref.shape[0]):
        o_ref[c] = q[:, c * LANES:(c + 1) * LANES].astype(o_ref.dtype)


def _q_proj(h, gains, n_gain, w_q, *, tm):
    m, d = h.shape
    da = w_q.shape[-1]
    tn = TN_PROJ
    nc = tn // LANES
    return pl.pallas_call(
        _q_body,
        out_shape=jax.ShapeDtypeStruct((da // LANES, m, LANES), BF16),
        grid=(m // tm, da // tn),
        in_specs=[
            pl.BlockSpec((tm, d), lambda i, j: (i, 0), pipeline_mode=pl.Buffered(1)),
            pl.BlockSpec((None, 1, d), lambda i, j: (n_gain, 0, 0)),
            pl.BlockSpec((None, d, tn), lambda i, j: (0, 0, j)),
        ],
        out_specs=pl.BlockSpec((nc, tm, LANES), lambda i, j: (j, i, 0)),
        scratch_shapes=[pltpu.VMEM((tm, d), BF16)],
        compiler_params=_params(("parallel", "arbitrary")),
        name="q_proj",
    )(h, gains, w_q)


def _resid_mm_body(h_ref, y_ref, w_ref, b_ref, g_ref, o_ref):
    k = pl.program_id(1)

    @pl.when(k == 0)
    def _():
        o_ref[...] = jnp.broadcast_to(b_ref[...], o_ref.shape)

    y = jnp.concatenate([y_ref[c] for c in range(y_ref.shape[0])], axis=1)
    o_ref[...] += _dot(y, w_ref[...].astype(BF16))

    @pl.when(k == pl.num_programs(1) - 1)
    def _():
        o_ref[...] = h_ref[...] + _rms(o_ref[...], g_ref[...])


def _resid_mm(h, y, w, b, gains, n_gain, *, tm):
    m, d = h.shape
    tk = TK_RESID
    nc = tk // LANES
    one = pl.Buffered(1)
    return pl.pallas_call(
        _resid_mm_body,
        out_shape=jax.ShapeDtypeStruct((m, d), F32),
        grid=(m // tm, y.shape[0] // nc),
        in_specs=[
            pl.BlockSpec((tm, d), lambda i, k: (i, 0), pipeline_mode=one),
            pl.BlockSpec((nc, tm, LANES), lambda i, k: (k, i, 0)),
            pl.BlockSpec((None, tk, d), lambda i, k: (0, k, 0)),
            pl.BlockSpec((1, d), lambda i, k: (0, 0)),
            pl.BlockSpec((None, 1, d), lambda i, k: (n_gain, 0, 0)),
        ],
        out_specs=pl.BlockSpec((tm, d), lambda i, k: (i, 0), pipeline_mode=one),
        compiler_params=_params(("parallel", "arbitrary")),
        name="resid_mm",
    )(h, y, w, b.reshape(1, d), gains)


def _ln_silu(y, g, b):
    mu = jnp.mean(y, axis=-1, keepdims=True)
    dlt = y - mu
    var = jnp.mean(dlt * dlt, axis=-1, keepdims=True)
    z = (dlt * lax.rsqrt(var + LN_EPS)) * g + b
    return z * jax.nn.sigmoid(z)


_CONV_COLS = 512
_HALO = 32


def _conv_prompt_body(halo_ref, u_ref, wdw_ref, bdw_ref, lng_ref, lnb_ref, o_ref, ubuf_ref, ybuf_ref,
                      wb_ref, *, tt, width):
    i = pl.program_id(1)
    d = u_ref.shape[-1]
    ubuf_ref[pl.ds(0, _HALO), :] = jnp.where(i > 0, halo_ref[...], 0.0)
    ubuf_ref[pl.ds(_HALO, tt), :] = u_ref[...]
    for w in range(width):
        wb_ref[w] = jnp.broadcast_to(wdw_ref[w:w + 1, :], (SUBLANES, d))
    off = _HALO - (width - 1)
    n_tiles = (off + width - 1 + SUBLANES - 1) // SUBLANES + 1

    def conv_chunk(r, carry):
        base = pl.multiple_of(r * SUBLANES, SUBLANES)
        sub = lax.broadcasted_iota(jnp.int32, (SUBLANES, _CONV_COLS), 0)
        for cb in range(d // _CONV_COLS):
            cs = slice(cb * _CONV_COLS, (cb + 1) * _CONV_COLS)
            tiles = [ubuf_ref[pl.ds(base + SUBLANES * a, SUBLANES), cs] for a in range(n_tiles)]
            acc = jnp.broadcast_to(bdw_ref[:, cs], (SUBLANES, _CONV_COLS))
            for s in range(SUBLANES):
                z = None
                for a in range(n_tiles):
                    w = SUBLANES * a + s - off
                    if not 0 <= w < width:
                        continue
                    src = tiles[a] if s == 0 else jnp.where(sub >= s, tiles[a], tiles[a + 1])
                    term = src * wb_ref[w, :, cs]
                    z = term if z is None else z + term
                acc = acc + (z if s == 0 else pltpu.roll(z, SUBLANES - s, 0))
            ybuf_ref[pl.ds(base, SUBLANES), cs] = acc
        return carry

    lax.fori_loop(0, tt // SUBLANES, conv_chunk, 0)

    def ln_chunk(r, carry):
        base = pl.multiple_of(r * 16, 16)
        y = _ln_silu(ybuf_ref[pl.ds(base, 16), :], lng_ref[...], lnb_ref[...]).astype(o_ref.dtype)
        for c in range(o_ref.shape[0]):
            o_ref[c, pl.ds(base, 16), :] = y[:, c * LANES:(c + 1) * LANES]
        return carry

    lax.fori_loop(0, tt // 16, ln_chunk, 0)


def _conv_prompt(u, n_seq, seq_len, w_dw, b_dw, ln_g, ln_b):
    d = u.shape[-1]
    width = w_dw.shape[1]
    tt = TT_CONV
    nt = seq_len // tt
    hb = tt // _HALO
    return pl.pallas_call(
        functools.partial(_conv_prompt_body, tt=tt, width=width),
        out_shape=jax.ShapeDtypeStruct((d // LANES, n_seq * seq_len, LANES), BF16),
        grid=(n_seq, nt),
        in_specs=[
            pl.BlockSpec((_HALO, d), lambda b, i: (jnp.maximum((b * nt + i) * hb - 1, 0), 0)),
            pl.BlockSpec((tt, d), lambda b, i: (b * nt + i, 0)),
            pl.BlockSpec((None, width, d), lambda b, i: (0, 0, 0)),
            pl.BlockSpec((1, d), lambda b, i: (0, 0)),
            pl.BlockSpec((1, d), lambda b, i: (0, 0)),
            pl.BlockSpec((1, d), lambda b, i: (0, 0)),
        ],
        out_specs=pl.BlockSpec((d // LANES, tt, LANES), lambda b, i: (0, b * nt + i, 0)),
        scratch_shapes=[pltpu.VMEM((tt + _HALO, d), F32), pltpu.VMEM((tt, d), F32),
                        pltpu.VMEM((width, SUBLANES, d), F32)],
        compiler_params=_params(("parallel", "parallel")),
        name="conv_prompt",
    )(u, u, w_dw, b_dw, ln_g, ln_b)


def _conv_sample_body(st_ref, u_ref, wdw_ref, bdw_ref, lng_ref, lnb_ref, y_ref, ns_ref, ubuf_ref,
                      *, ts, width):
    hist = width - 1
    ubuf_ref[pl.ds(0, hist), :] = st_ref[...]
    ubuf_ref[pl.ds(hist, ts), :] = u_ref[...]
    full = ubuf_ref[...]
    acc = jnp.broadcast_to(bdw_ref[...], (ts, full.shape[-1]))
    for w in range(width):
        acc = acc + full[w:w + ts] * wdw_ref[w:w + 1, :]
    y_ref[...] = _ln_silu(acc, lng_ref[...], lnb_ref[...])
    ns_ref[...] = full[ts:ts + hist]


def _conv_sample(u_s, state, w_dw, b_dw, ln_g, ln_b):
    bs, ts, d = u_s.shape
    width = w_dw.shape[1]
    hist = width - 1
    return pl.pallas_call(
        functools.partial(_conv_sample_body, ts=ts, width=width),
        out_shape=(jax.ShapeDtypeStruct((bs, ts, d), F32),
                   jax.ShapeDtypeStruct((1, bs, hist, d), F32)),
        grid=(bs,),
        in_specs=[
            pl.BlockSpec((None, None, hist, d), lambda b: (0, b, 0, 0)),
            pl.BlockSpec((None, ts, d), lambda b: (b, 0, 0)),
            pl.BlockSpec((None, width, d), lambda b: (0, 0, 0)),
            pl.BlockSpec((1, d), lambda b: (0, 0)),
            pl.BlockSpec((1, d), lambda b: (0, 0)),
            pl.BlockSpec((1, d), lambda b: (0, 0)),
        ],
        out_specs=(pl.BlockSpec((None, ts, d), lambda b: (b, 0, 0)),
                   pl.BlockSpec((None, None, hist, d), lambda b: (0, b, 0, 0))),
        scratch_shapes=[pltpu.VMEM((hist + ts, d), F32)],
        compiler_params=_params(("parallel",)),
        name="conv_sample",
    )(state, u_s, w_dw, b_dw, ln_g, ln_b)


def _cumsum_body(x_ref, ct_ref, carry_ref, *, tc):
    @pl.when(pl.program_id(1) == 0)
    def _():
        carry_ref[...] = jnp.zeros_like(carry_ref)

    row = lax.broadcasted_iota(jnp.int32, (tc, tc), 0)
    col = lax.broadcasted_iota(jnp.int32, (tc, tc), 1)
    upper = _ones_where(row <= col)
    ctb = carry_ref[...]
    for piece in _split3(x_ref[...]):
        ctb = ctb + lax.dot_general(piece, upper, _TN, preferred_element_type=F32)
    for h in range(ct_ref.shape[0]):
        ct_ref[h] = ctb[h:h + 1, :]
    carry_ref[...] = ctb[:, tc - 1:tc]


def _cumsum_prompt(logf, n_seq, seq_len):
    nh = logf.shape[-1]
    tc = TC_CUMSUM
    nt = seq_len // tc
    return pl.pallas_call(
        functools.partial(_cumsum_body, tc=tc),
        out_shape=jax.ShapeDtypeStruct((n_seq, nh, 1, seq_len), F32),
        grid=(n_seq, nt),
        in_specs=[pl.BlockSpec((tc, nh), lambda b, j: (b * nt + j, 0))],
        out_specs=pl.BlockSpec((None, nh, 1, tc), lambda b, j: (b, 0, 0, j)),
        scratch_shapes=[pltpu.VMEM((nh, 1), F32)],
        compiler_params=_params(("parallel", "arbitrary")),
        name="cumsum_prompt",
    )(logf)


def _attn_prompt_body(q_ref, k_ref, v_ref, ctq_ref, ctk_ref, o_ref, m_ref, l_ref, acc_ref, cq_ref,
                      *, scale2):
    qi = pl.program_id(1)
    ki = pl.program_id(2)
    n_heads, tq, dh = q_ref.shape
    tk = k_ref.shape[1]
    reps = tk // dh

    @pl.when(ki == 0)
    def _():
        m_ref[...] = jnp.full_like(m_ref, NEG_INF)
        l_ref[...] = jnp.zeros_like(l_ref)
        acc_ref[...] = jnp.zeros_like(acc_ref)

        def init_head(h, carry):
            row = ctq_ref[h] * LOG2E
            cq_ref[h] = jnp.broadcast_to(row, (dh, tq)).T
            return carry

        lax.fori_loop(0, n_heads, init_head, 0)

    def run(diagonal):
        def head(h, carry):
            s = lax.dot_general(q_ref[h], k_ref[h], _NT, preferred_element_type=F32) * scale2
            s = s + _lane_tile(cq_ref[h], reps) - ctk_ref[h] * LOG2E
            if diagonal:
                row = lax.broadcasted_iota(jnp.int32, (tq, tk), 0)
                col = lax.broadcasted_iota(jnp.int32, (tq, tk), 1)
                s = jnp.where(col <= row, s, NEG_INF)
            m_prev = m_ref[h]
            m_new = jnp.maximum(m_prev, jnp.max(s, axis=-1, keepdims=True))
            alpha = jnp.exp2(m_prev - m_new)
            p = jnp.exp2(s - _lane_tile(m_new, reps))
            l_ref[h] = alpha * l_ref[h] + jnp.sum(p, axis=-1, keepdims=True)
            acc_ref[h] = alpha * acc_ref[h] + _dot(p.astype(BF16), v_ref[h])
            m_ref[h] = m_new
            return carry

        lax.fori_loop(0, n_heads, head, 0)

    @pl.when(ki < qi)
    def _():
        run(False)

    @pl.when(ki == qi)
    def _():
        run(True)

        def finish(h, carry):
            o_ref[h] = (acc_ref[h] / l_ref[h]).astype(o_ref.dtype)
            return carry

        lax.fori_loop(0, n_heads, finish, 0)


def _attn_prompt(q, k, v, ct, n_seq, seq_len):
    n_heads, _, dh = q.shape
    tq = TQ_ATTN
    nq = seq_len // tq
    stat = pltpu.VMEM((n_heads, tq, dh), F32)
    kv_map = lambda b, qi, ki: (0, b * nq + jnp.minimum(ki, qi), 0)
    return pl.pallas_call(
        functools.partial(_attn_prompt_body, scale2=dh ** -0.5 * LOG2E),
        out_shape=jax.ShapeDtypeStruct((n_heads, n_seq * seq_len, dh), BF16),
        grid=(n_seq, nq, nq),
        in_specs=[
            pl.BlockSpec((n_heads, tq, dh), lambda b, qi, ki: (0, b * nq + qi, 0)),
            pl.BlockSpec((n_heads, tq, dh), kv_map),
            pl.BlockSpec((n_heads, tq, dh), kv_map),
            pl.BlockSpec((None, n_heads, 1, tq), lambda b, qi, ki: (b, 0, 0, qi)),
            pl.BlockSpec((None, n_heads, 1, tq), lambda b, qi, ki: (b, 0, 0, jnp.minimum(ki, qi))),
        ],
        out_specs=pl.BlockSpec((n_heads, tq, dh), lambda b, qi, ki: (0, b * nq + qi, 0)),
        scratch_shapes=[stat, stat, stat, stat],
        compiler_params=_params(("parallel", "parallel", "arbitrary")),
        name="attn_prompt",
    )(q, k, v, ct, ct)


def _lane_iota(shape):
    return lax.broadcasted_iota(jnp.int32, shape, len(shape) - 1)


def _suffix_sums_page(x, n_heads):
    n_rows, n_lanes = x.shape
    lane = _lane_iota(x.shape)
    sub = lax.broadcasted_iota(jnp.int32, x.shape, 0)
    y = x
    sh = n_heads
    while sh < n_lanes:
        y = y + jnp.where(lane + sh < n_lanes, pltpu.roll(y, n_lanes - sh, 1), 0.0)
        sh *= 2
    z = jnp.where(lane < n_heads, y, 0.0)
    sh = n_heads
    while sh < n_lanes:
        z = z + pltpu.roll(z, sh, 1)
        sh *= 2
    zi = z
    sh = 1
    while sh < n_rows:
        zi = zi + jnp.where(sub + sh < n_rows, pltpu.roll(zi, n_rows - sh, 0), 0.0)
        sh *= 2
    return y + (zi - z), zi[0:1, :]


def _attn_sample_body(pt_ref, q_ref, kn_ref, vn_ref, lfn_ref, *refs, n_heads, ts, scale2):
    npg = PAGES_PER_STEP
    k_refs = refs[0:npg]
    v_refs = refs[npg:2 * npg]
    lf_refs = refs[2 * npg:3 * npg]
    o_ref, mask_ref, m_ref, l_ref, acc_ref, tail_ref, cq_ref = refs[3 * npg:]
    del pt_ref
    j = pl.program_id(1)
    rows, dh = q_ref.shape
    page, n_groups, hg, _ = k_refs[0].shape
    rg = rows // n_groups
    n_keys = page * hg

    def attend(s, v_bf16, state):
        rmax = jnp.max(s, axis=-1, keepdims=True)
        m_new = rmax if state is None else jnp.maximum(state[0], rmax)
        p = jnp.exp2(s - m_new)
        psum = jnp.sum(p, axis=-1, keepdims=True)
        pv = _dot(p.astype(BF16), v_bf16)
        if state is None:
            return m_new, psum, pv
        alpha = jnp.exp2(state[0] - m_new)
        return m_new, alpha * state[1] + psum, alpha * state[2] + pv

    @pl.when(j == 0)
    def _():
        r_i = lax.broadcasted_iota(jnp.int32, (rg, n_keys), 0)
        l_i = _lane_iota((rg, n_keys))
        mask_ref[...] = jnp.where(l_i % hg == r_i // ts, 0.0, NEG_INF)

        n_new = ts * n_heads
        cn = jnp.broadcast_to(lfn_ref[...], (8, n_new))
        lane8 = _lane_iota((8, n_new))
        sh = n_heads
        while sh < n_new:
            cn = cn + jnp.where(lane8 >= sh, pltpu.roll(cn, sh, 1), 0.0)
            sh *= 2
        cn_row = cn[0:1, :]
        r2 = lax.broadcasted_iota(jnp.int32, (rows, n_new), 0)
        l2 = _lane_iota((rows, n_new))
        own = l2 == (r2 % ts) * n_heads + r2 // ts
        cq = jnp.sum(jnp.where(own, jnp.broadcast_to(cn_row, (rows, n_new)), 0.0),
                     axis=-1, keepdims=True)
        cq_ref[...] = cq * LOG2E
        tail_ref[...] = jnp.zeros_like(tail_ref)

        s = lax.dot_general(q_ref[...], kn_ref[...].astype(BF16), _NT,
                            preferred_element_type=F32) * scale2
        s = s + (cq - cn_row) * LOG2E
        valid = (l2 % n_heads == r2 // ts) & (l2 // n_heads <= r2 % ts)
        s = jnp.where(valid, s, NEG_INF)
        m, l, acc = attend(s, vn_ref[...].astype(BF16), None)
        m_ref[...] = m
        l_ref[...] = l
        acc_ref[...] = acc

    for g in range(n_groups):
        rs = pl.ds(g * rg, rg)
        tail = tail_ref[g, 0:1, :]
        e_tiles = [None] * npg
        for i in reversed(range(npg)):
            x = lf_refs[i][g]
            incl, tot = _suffix_sums_page(x, hg)
            e_tiles[i] = (incl - x + tail) * LOG2E
            tail = tail + tot
        tail_ref[g] = jnp.broadcast_to(tail, tail_ref.shape[1:])
        e_row = jnp.concatenate([e[a:a + 1, :] for e in e_tiles for a in range(e.shape[0])], axis=1)
        bias = jnp.concatenate([cq_ref[rs, :] + mask_ref[...]] * npg, axis=1) + e_row

        k_g = jnp.concatenate([r[:, g].reshape(n_keys, dh).astype(BF16) for r in k_refs], axis=0)
        v_g = jnp.concatenate([r[:, g].reshape(n_keys, dh).astype(BF16) for r in v_refs], axis=0)
        s = lax.dot_general(q_ref[rs, :], k_g, _NT, preferred_element_type=F32) * scale2
        state = attend(s + bias, v_g, (m_ref[rs, :], l_ref[rs, :], acc_ref[rs, :]))
        m_ref[rs, :], l_ref[rs, :], acc_ref[rs, :] = state

    @pl.when(j == pl.num_programs(1) - 1)
    def _():
        o_ref[...] = acc_ref[...] / l_ref[...]


def _attn_sample(q2, kn2, vn2, lfn, cache_k, cache_v, cache_logf, page_table, n_heads, ts):
    bs, rows, dh = q2.shape
    n_phys, page = cache_k.shape[0], cache_k.shape[1]
    n_pages = page_table.shape[1]
    npg = PAGES_PER_STEP
    n_steps = n_pages // npg
    hg = SUBLANES
    n_groups = n_heads // hg
    n_keys = page * hg
    lanes = kn2.shape[1]
    ck = cache_k.reshape(n_phys, page, n_groups, hg, dh)
    cv = cache_v.reshape(n_phys, page, n_groups, hg, dh)
    lfc = cache_logf.reshape(n_phys, page, n_groups, hg).transpose(0, 2, 1, 3).reshape(
        n_phys, n_groups, n_keys // lanes, lanes)

    def page_map(i, nd):
        return lambda b, j, pt: (pt[b, (n_steps - 1 - j) * npg + i],) + (0,) * nd

    per_seq = lambda b, j, pt: (b, 0, 0)
    in_specs = [pl.BlockSpec((None, rows, dh), per_seq),
                pl.BlockSpec((None, lanes, dh), per_seq),
                pl.BlockSpec((None, lanes, dh), per_seq),
                pl.BlockSpec((None, 1, lanes), per_seq)]
    in_specs += [pl.BlockSpec((None, page, n_groups, hg, dh), page_map(i, 4)) for i in range(npg)]
    in_specs += [pl.BlockSpec((None, page, n_groups, hg, dh), page_map(i, 4)) for i in range(npg)]
    in_specs += [pl.BlockSpec((None, n_groups, n_keys // lanes, lanes), page_map(i, 3))
                 for i in range(npg)]
    grid_spec = pltpu.PrefetchScalarGridSpec(
        num_scalar_prefetch=1,
        grid=(bs, n_steps),
        in_specs=in_specs,
        out_specs=pl.BlockSpec((None, rows, dh), per_seq),
        scratch_shapes=[pltpu.VMEM((rows // n_groups, n_keys), F32), pltpu.VMEM((rows, 1), F32),
                        pltpu.VMEM((rows, 1), F32), pltpu.VMEM((rows, dh), F32),
                        pltpu.VMEM((n_groups, SUBLANES, lanes), F32), pltpu.VMEM((rows, 1), F32)],
    )
    return pl.pallas_call(
        functools.partial(_attn_sample_body, n_heads=n_heads, ts=ts, scale2=dh ** -0.5 * LOG2E),
        out_shape=jax.ShapeDtypeStruct((bs, rows, dh), F32),
        grid_spec=grid_spec,
        compiler_params=_params(("parallel", "arbitrary")),
        name="attn_sample",
    )(page_table, q2, kn2, vn2, lfn, *([ck] * npg), *([cv] * npg), *([lfc] * npg))


def _to_blocks(x):
    rows, c = x.shape
    return x.reshape(rows, c // LANES, LANES).transpose(1, 0, 2)


def kernel(x_prompt, x_sample, state_conv, cache_k, cache_v, cache_logf, page_table, norm_gain,
           ffn1_w_gate, ffn1_w_up, ffn1_w_down, ffn2_w_gate, ffn2_w_up, ffn2_w_down,
           conv_w_pw1, conv_b_pw1, conv_w_dw, conv_b_dw, conv_ln_g, conv_ln_b, conv_w_pw2,
           conv_b_pw2, kv_norm_g, w_k, w_v, w_fgate, b_fgate, attn_w_q, attn_w_o):
    n_seq, seq_len, d = x_prompt.shape
    bs, ts, _ = x_sample.shape
    depth = norm_gain.shape[0]
    n_heads = w_fgate.shape[-1]
    dh = w_k.shape[-1] // n_heads
    hist = conv_w_dw.shape[1] - 1
    assert depth == 2 and conv_w_pw1.shape[0] == 1 and attn_w_q.shape[0] == 1
    assert seq_len >= hist and ts * n_heads == LANES and dh == LANES
    mp, ms = n_seq * seq_len, bs * ts
    gains = norm_gain.reshape(depth * N_NORMS, 1, d)
    zero_bias = jnp.zeros((d,), F32)

    def trunk_a(h, tm):
        h = _ffn(h, gains, 0, 1, ffn1_w_gate, ffn1_w_up, ffn1_w_down, 0, tm=tm)
        return h, _glu_proj(h, gains, 2, conv_w_pw1, conv_b_pw1, tm=tm)

    def trunk_b(h, y, tm, blocked):
        h = _resid_mm(h, y, conv_w_pw2, conv_b_pw2[0], gains, 3, tm=tm)
        h = _ffn(h, gains, 4, 5, ffn2_w_gate, ffn2_w_up, ffn2_w_down, 0, tm=tm)
        kv = _kv_proj(h, kv_norm_g, w_k, w_v, w_fgate, b_fgate, tm=tm, blocked=blocked)
        h = _ffn(h, gains, N_NORMS + 0, N_NORMS + 1, ffn1_w_gate, ffn1_w_up, ffn1_w_down, 1, tm=tm)
        return h, kv, _q_proj(h, gains, N_NORMS + 2, attn_w_q, tm=tm)

    def trunk_c(h, o, tm):
        h = _resid_mm(h, o, attn_w_o, zero_bias, gains, N_NORMS + 3, tm=tm)
        return _ffn(h, gains, N_NORMS + 4, N_NORMS + 5, ffn2_w_gate, ffn2_w_up, ffn2_w_down, 1,
                    tm=tm)

    hp, u_p = trunk_a(x_prompt.reshape(mp, d), TM_PROMPT)
    y_p = _conv_prompt(u_p, n_seq, seq_len, conv_w_dw, conv_b_dw, conv_ln_g, conv_ln_b)
    hp, (k_p, v_p, lf_p, kb_p, vb_p), q_p = trunk_b(hp, y_p, TM_PROMPT, True)
    ct = _cumsum_prompt(lf_p, n_seq, seq_len)
    o_p = _attn_prompt(q_p, kb_p, vb_p, ct, n_seq, seq_len)
    hp = trunk_c(hp, o_p, TM_PROMPT)
    new_conv_p = u_p.reshape(n_seq, seq_len, d)[None, :, seq_len - hist:, :]

    hs, u_s = trunk_a(x_sample.reshape(ms, d), ms)
    y_s, new_conv_s = _conv_sample(u_s.reshape(bs, ts, d), state_conv, conv_w_dw, conv_b_dw,
                                   conv_ln_g, conv_ln_b)
    hs, (k_s, v_s, lf_s), q_s = trunk_b(hs, _to_blocks(y_s.reshape(ms, d)).astype(BF16), ms, False)
    q2 = q_s.reshape(n_heads, bs, ts, dh).transpose(1, 0, 2, 3).reshape(bs, n_heads * ts, dh)
    o2 = _attn_sample(q2, k_s.reshape(bs, ts * n_heads, dh), v_s.reshape(bs, ts * n_heads, dh),
                      lf_s.reshape(bs, 1, ts * n_heads), cache_k, cache_v, cache_logf,
                      page_table, n_heads, ts)
    o_s = o2.reshape(bs, n_heads, ts, dh).transpose(1, 0, 2, 3).reshape(n_heads, ms, dh)
    hs = trunk_c(hs, o_s.astype(BF16), ms)

    return (hp.reshape(n_seq, seq_len, d), hs.reshape(bs, ts, d),
            new_conv_p,
            k_p.reshape(n_seq, seq_len, n_heads, dh), v_p.reshape(n_seq, seq_len, n_heads, dh),
            lf_p.reshape(n_seq, seq_len, n_heads),
            new_conv_s,
            k_s.reshape(bs, ts, n_heads, dh), v_s.reshape(bs, ts, n_heads, dh),
            lf_s.reshape(bs, ts, n_heads))
```

```python
import functools
import math

import jax
import jax.numpy as jnp
from jax import lax
from jax.experimental import pallas as pl
from jax.experimental.pallas import tpu as pltpu

F32 = jnp.float32
BF16 = jnp.bfloat16

RMS_EPS = 1e-6
LN_EPS = 1e-5
NEG_INF = -1e30
LOG2E = math.log2(math.e)
N_NORMS = 6
LANES = 128
SUBLANES = 8
PAGES_PER_STEP = 8

VMEM_LIMIT_BYTES = 56 * 1024 * 1024

TM_PROMPT = 1024
TF_FFN = 512
TN_PROJ = 512
TK_RESID = 512
TT_CONV = 256
TQ_ATTN = 512
TC_CUMSUM = 256

_NT = (((1,), (1,)), ((), ()))
_TN = (((0,), (0,)), ((), ()))


def _params(semantics):
    return pltpu.CompilerParams(dimension_semantics=semantics, vmem_limit_bytes=VMEM_LIMIT_BYTES)


def _rms(x, g):
    ms = jnp.mean(x * x, axis=-1, keepdims=True)
    return (x * lax.rsqrt(ms + RMS_EPS)) * g


def _dot(a, b):
    return jnp.dot(a, b, preferred_element_type=F32)


def _split3(x):
    hi = x.astype(BF16)
    r1 = x - hi.astype(F32)
    mid = r1.astype(BF16)
    lo = (r1 - mid.astype(F32)).astype(BF16)
    return hi, mid, lo


def _ones_where(mask):
    return jnp.where(mask, 1.0, 0.0).astype(BF16)


def _lane_tile(x, reps):
    return jnp.concatenate([x] * reps, axis=1)


def _ffn_body(x_ref, gpre_ref, gpost_ref, wg_ref, wu_ref, wd_ref, o_ref, xn_ref, *, d_ff, tf):
    f = pl.program_id(1)

    @pl.when(f == 0)
    def _():
        xn_ref[...] = _rms(x_ref[...], gpre_ref[...]).astype(BF16)
        o_ref[...] = jnp.zeros_like(o_ref)

    xn = xn_ref[...]
    gate = _dot(xn, wg_ref[...].astype(BF16))
    up = _dot(xn, wu_ref[...].astype(BF16))
    act = gate * jax.nn.sigmoid(gate) * up
    col = f * tf + lax.broadcasted_iota(jnp.int32, (1, tf), 1)
    act = jnp.where(col < d_ff, act, 0.0).astype(BF16)
    row = f * tf + lax.broadcasted_iota(jnp.int32, (tf, 1), 0)
    wd = jnp.where(row < d_ff, wd_ref[...], 0.0).astype(BF16)
    o_ref[...] += _dot(act, wd)

    @pl.when(f == pl.num_programs(1) - 1)
    def _():
        o_ref[...] = x_ref[...] + 0.5 * _rms(o_ref[...], gpost_ref[...])


def _ffn_specs(d, tm, tf, n_pre, n_post, layer):
    one = pl.Buffered(1)
    in_specs = [
        pl.BlockSpec((tm, d), lambda i, f, *_: (i, 0), pipeline_mode=one),
        pl.BlockSpec((None, 1, d), lambda i, f, *_: (n_pre, 0, 0)),
        pl.BlockSpec((None, 1, d), lambda i, f, *_: (n_post, 0, 0)),
        pl.BlockSpec((None, d, tf), lambda i, f, *_: (layer, 0, f)),
        pl.BlockSpec((None, d, tf), lambda i, f, *_: (layer, 0, f)),
        pl.BlockSpec((None, tf, d), lambda i, f, *_: (layer, f, 0)),
    ]
    return in_specs, pl.BlockSpec((tm, d), lambda i, f, *_: (i, 0), pipeline_mode=one)


def _ffn(h, gains, n_pre, n_post, w_gate, w_up, w_down, layer, *, tm):
    m, d = h.shape
    d_ff = w_gate.shape[-1]
    tf = TF_FFN
    in_specs, out_spec = _ffn_specs(d, tm, tf, n_pre, n_post, layer)
    return pl.pallas_call(
        functools.partial(_ffn_body, d_ff=d_ff, tf=tf),
        out_shape=jax.ShapeDtypeStruct((m, d), F32),
        grid=(m // tm, pl.cdiv(d_ff, tf)),
        in_specs=in_specs,
        out_specs=out_spec,
        scratch_shapes=[pltpu.VMEM((tm, d), BF16)],
        compiler_params=_params(("parallel", "arbitrary")),
        name="ffn",
    )(h, gains, gains, w_gate, w_up, w_down)


def _glu_body(x_ref, g_ref, wa_ref, wg_ref, ba_ref, bg_ref, o_ref, xn_ref):
    @pl.when(pl.program_id(1) == 0)
    def _():
        xn_ref[...] = _rms(x_ref[...], g_ref[...]).astype(BF16)

    xn = xn_ref[...]
    a = _dot(xn, wa_ref[...].astype(BF16)) + ba_ref[...]
    gate = _dot(xn, wg_ref[...].astype(BF16)) + bg_ref[...]
    o_ref[...] = a * jax.nn.sigmoid(gate)


def _glu_proj(h, gains, n_gain, w_pw1, b_pw1, *, tm):
    m, d = h.shape
    dc = w_pw1.shape[-1] // 2
    tn = TN_PROJ
    nj = dc // tn
    b3 = b_pw1.reshape(b_pw1.shape[0], 1, 2 * dc)
    return pl.pallas_call(
        _glu_body,
        out_shape=jax.ShapeDtypeStruct((m, dc), F32),
        grid=(m // tm, nj),
        in_specs=[
            pl.BlockSpec((tm, d), lambda i, j: (i, 0), pipeline_mode=pl.Buffered(1)),
            pl.BlockSpec((None, 1, d), lambda i, j: (n_gain, 0, 0)),
            pl.BlockSpec((None, d, tn), lambda i, j: (0, 0, j)),
            pl.BlockSpec((None, d, tn), lambda i, j: (0, 0, j + nj)),
            pl.BlockSpec((None, 1, tn), lambda i, j: (0, 0, j)),
            pl.BlockSpec((None, 1, tn), lambda i, j: (0, 0, j + nj)),
        ],
        out_specs=pl.BlockSpec((tm, tn), lambda i, j: (i, j)),
        scratch_shapes=[pltpu.VMEM((tm, d), BF16)],
        compiler_params=_params(("parallel", "arbitrary")),
        name="glu_proj",
    )(h, gains, w_pw1, w_pw1, b3, b3)


def _kv_body(x_ref, g_ref, wk_ref, wv_ref, wf_ref, bf_ref, k_ref, v_ref, lf_ref, *rest, blocked):
    xn_ref = rest[-1]

    @pl.when(pl.program_id(1) == 0)
    def _():
        xn = _rms(x_ref[...], g_ref[...]).astype(BF16)
        xn_ref[...] = xn
        z = _dot(xn, wf_ref[...].astype(BF16)) + bf_ref[...]
        lf_ref[...] = jnp.minimum(z, 0.0) - jnp.log1p(jnp.exp(-jnp.abs(z)))

    xn = xn_ref[...]
    k = _dot(xn, wk_ref[...].astype(BF16))
    v = _dot(xn, wv_ref[...].astype(BF16))
    k_ref[...] = k
    v_ref[...] = v
    if blocked:
        kb_ref, vb_ref = rest[0], rest[1]
        for c in range(kb_ref.shape[0]):
            cs = slice(c * LANES, (c + 1) * LANES)
            kb_ref[c] = k[:, cs].astype(BF16)
            vb_ref[c] = v[:, cs].astype(BF16)


def _kv_proj(h, g, w_k, w_v, w_f, b_f, *, tm, blocked):
    m, d = h.shape
    da = w_k.shape[-1]
    nh = w_f.shape[-1]
    tn = TN_PROJ
    nc = tn // LANES
    out_shape = [jax.ShapeDtypeStruct((m, da), F32), jax.ShapeDtypeStruct((m, da), F32),
                 jax.ShapeDtypeStruct((m, nh), F32)]
    out_specs = [pl.BlockSpec((tm, tn), lambda i, j: (i, j)),
                 pl.BlockSpec((tm, tn), lambda i, j: (i, j)),
                 pl.BlockSpec((tm, nh), lambda i, j: (i, 0))]
    if blocked:
        out_shape += [jax.ShapeDtypeStruct((da // LANES, m, LANES), BF16)] * 2
        out_specs += [pl.BlockSpec((nc, tm, LANES), lambda i, j: (j, i, 0))] * 2
    return pl.pallas_call(
        functools.partial(_kv_body, blocked=blocked),
        out_shape=tuple(out_shape),
        grid=(m // tm, da // tn),
        in_specs=[
            pl.BlockSpec((tm, d), lambda i, j: (i, 0), pipeline_mode=pl.Buffered(1)),
            pl.BlockSpec((1, d), lambda i, j: (0, 0)),
            pl.BlockSpec((d, tn), lambda i, j: (0, j)),
            pl.BlockSpec((d, tn), lambda i, j: (0, j)),
            pl.BlockSpec((d, nh), lambda i, j: (0, 0)),
            pl.BlockSpec((1, nh), lambda i, j: (0, 0)),
        ],
        out_specs=tuple(out_specs),
        scratch_shapes=[pltpu.VMEM((tm, d), BF16)],
        compiler_params=_params(("parallel", "arbitrary")),
        name="kv_proj",
    )(h, g.reshape(1, d), w_k, w_v, w_f, b_f.reshape(1, nh))


def _q_body(x_ref, g_ref, w_ref, o_ref, xn_ref):
    @pl.when(pl.program_id(1) == 0)
    def _():
        xn_ref[...] = _rms(x_ref[...], g_ref[...]).astype(BF16)

    q = _dot(xn_ref[...], w_ref[...].astype(BF16))
    for c in range(o_ref.shape[0]):
        o_ref[c] = q[:, c * LANES:(c + 1) * LANES].astype(o_ref.dtype)


def _q_proj(h, gains, n_gain, w_q, *, tm):
    m, d = h.shape
    da = w_q.shape[-1]
    tn = TN_PROJ
    nc = tn // LANES
    return pl.pallas_call(
        _q_body,
        out_shape=jax.ShapeDtypeStruct((da // LANES, m, LANES), BF16),
        grid=(m // tm, da // tn),
        in_specs=[
            pl.BlockSpec((tm, d), lambda i, j: (i, 0), pipeline_mode=pl.Buffered(1)),
            pl.BlockSpec((None, 1, d), lambda i, j: (n_gain, 0, 0)),
            pl.BlockSpec((None, d, tn), lambda i, j: (0, 0, j)),
        ],
        out_specs=pl.BlockSpec((nc, tm, LANES), lambda i, j: (j, i, 0)),
        scratch_shapes=[pltpu.VMEM((tm, d), BF16)],
        compiler_params=_params(("parallel", "arbitrary")),
        name="q_proj",
    )(h, gains, w_q)


def _resid_mm_body(h_ref, y_ref, w_ref, b_ref, g_ref, o_ref):
    k = pl.program_id(1)

    @pl.when(k == 0)
    def _():
        o_ref[...] = jnp.broadcast_to(b_ref[...], o_ref.shape)

    y = jnp.concatenate([y_ref[c] for c in range(y_ref.shape[0])], axis=1)
    o_ref[...] += _dot(y, w_ref[...].astype(BF16))

    @pl.when(k == pl.num_programs(1) - 1)
    def _():
        o_ref[...] = h_ref[...] + _rms(o_ref[...], g_ref[...])


def _resid_mm(h, y, w, b, gains, n_gain, *, tm):
    m, d = h.shape
    tk = TK_RESID
    nc = tk // LANES
    one = pl.Buffered(1)
    return pl.pallas_call(
        _resid_mm_body,
        out_shape=jax.ShapeDtypeStruct((m, d), F32),
        grid=(m // tm, y.shape[0] // nc),
        in_specs=[
            pl.BlockSpec((tm, d), lambda i, k: (i, 0), pipeline_mode=one),
            pl.BlockSpec((nc, tm, LANES), lambda i, k: (k, i, 0)),
            pl.BlockSpec((None, tk, d), lambda i, k: (0, k, 0)),
            pl.BlockSpec((1, d), lambda i, k: (0, 0)),
            pl.BlockSpec((None, 1, d), lambda i, k: (n_gain, 0, 0)),
        ],
        out_specs=pl.BlockSpec((tm, d), lambda i, k: (i, 0), pipeline_mode=one),
        compiler_params=_params(("parallel", "arbitrary")),
        name="resid_mm",
    )(h, y, w, b.reshape(1, d), gains)


def _ln_silu(y, g, b):
    mu = jnp.mean(y, axis=-1, keepdims=True)
    dlt = y - mu
    var = jnp.mean(dlt * dlt, axis=-1, keepdims=True)
    z = (dlt * lax.rsqrt(var + LN_EPS)) * g + b
    return z * jax.nn.sigmoid(z)


_CONV_COLS = 512
_HALO = 32


def _conv_prompt_body(halo_ref, u_ref, wdw_ref, bdw_ref, lng_ref, lnb_ref, o_ref, ubuf_ref, ybuf_ref,
                      wb_ref, *, tt, width):
    i = pl.program_id(1)
    d = u_ref.shape[-1]
    ubuf_ref[pl.ds(0, _HALO), :] = jnp.where(i > 0, halo_ref[...], 0.0)
    ubuf_ref[pl.ds(_HALO, tt), :] = u_ref[...]
    for w in range(width):
        wb_ref[w] = jnp.broadcast_to(wdw_ref[w:w + 1, :], (SUBLANES, d))
    off = _HALO - (width - 1)
    n_tiles = (off + width - 1 + SUBLANES - 1) // SUBLANES + 1

    def conv_chunk(r, carry):
        base = pl.multiple_of(r * SUBLANES, SUBLANES)
        sub = lax.broadcasted_iota(jnp.int32, (SUBLANES, _CONV_COLS), 0)
        for cb in range(d // _CONV_COLS):
            cs = slice(cb * _CONV_COLS, (cb + 1) * _CONV_COLS)
            tiles = [ubuf_ref[pl.ds(base + SUBLANES * a, SUBLANES), cs] for a in range(n_tiles)]
            acc = jnp.broadcast_to(bdw_ref[:, cs], (SUBLANES, _CONV_COLS))
            for s in range(SUBLANES):
                z = None
                for a in range(n_tiles):
                    w = SUBLANES * a + s - off
                    if not 0 <= w < width:
                        continue
                    src = tiles[a] if s == 0 else jnp.where(sub >= s, tiles[a], tiles[a + 1])
                    term = src * wb_ref[w, :, cs]
                    z = term if z is None else z + term
                acc = acc + (z if s == 0 else pltpu.roll(z, SUBLANES - s, 0))
            ybuf_ref[pl.ds(base, SUBLANES), cs] = acc
        return carry

    lax.fori_loop(0, tt // SUBLANES, conv_chunk, 0)

    def ln_chunk(r, carry):
        base = pl.multiple_of(r * 16, 16)
        y = _ln_silu(ybuf_ref[pl.ds(base, 16), :], lng_ref[...], lnb_ref[...]).astype(o_ref.dtype)
        for c in range(o_ref.shape[0]):
            o_ref[c, pl.ds(base, 16), :] = y[:, c * LANES:(c + 1) * LANES]
        return carry

    lax.fori_loop(0, tt // 16, ln_chunk, 0, unroll=2)


def _conv_prompt(u, n_seq, seq_len, w_dw, b_dw, ln_g, ln_b):
    d = u.shape[-1]
    width = w_dw.shape[1]
    tt = TT_CONV
    nt = seq_len // tt
    hb = tt // _HALO
    return pl.pallas_call(
        functools.partial(_conv_prompt_body, tt=tt, width=width),
        out_shape=jax.ShapeDtypeStruct((d // LANES, n_seq * seq_len, LANES), BF16),
        grid=(n_seq, nt),
        in_specs=[
            pl.BlockSpec((_HALO, d), lambda b, i: (jnp.maximum((b * nt + i) * hb - 1, 0), 0)),
            pl.BlockSpec((tt, d), lambda b, i: (b * nt + i, 0)),
            pl.BlockSpec((None, width, d), lambda b, i: (0, 0, 0)),
            pl.BlockSpec((1, d), lambda b, i: (0, 0)),
            pl.BlockSpec((1, d), lambda b, i: (0, 0)),
            pl.BlockSpec((1, d), lambda b, i: (0, 0)),
        ],
        out_specs=pl.BlockSpec((d // LANES, tt, LANES), lambda b, i: (0, b * nt + i, 0)),
        scratch_shapes=[pltpu.VMEM((tt + _HALO, d), F32), pltpu.VMEM((tt, d), F32),
                        pltpu.VMEM((width, SUBLANES, d), F32)],
        compiler_params=_params(("parallel", "parallel")),
        name="conv_prompt",
    )(u, u, w_dw, b_dw, ln_g, ln_b)


def _conv_sample_body(st_ref, u_ref, wdw_ref, bdw_ref, lng_ref, lnb_ref, y_ref, ns_ref, ubuf_ref,
                      *, ts, width):
    hist = width - 1
    ubuf_ref[pl.ds(0, hist), :] = st_ref[...]
    ubuf_ref[pl.ds(hist, ts), :] = u_ref[...]
    full = ubuf_ref[...]
    acc = jnp.broadcast_to(bdw_ref[...], (ts, full.shape[-1]))
    for w in range(width):
        acc = acc + full[w:w + ts] * wdw_ref[w:w + 1, :]
    y_ref[...] = _ln_silu(acc, lng_ref[...], lnb_ref[...])
    ns_ref[...] = full[ts:ts + hist]


def _conv_sample(u_s, state, w_dw, b_dw, ln_g, ln_b):
    bs, ts, d = u_s.shape
    width = w_dw.shape[1]
    hist = width - 1
    return pl.pallas_call(
        functools.partial(_conv_sample_body, ts=ts, width=width),
        out_shape=(jax.ShapeDtypeStruct((bs, ts, d), F32),
                   jax.ShapeDtypeStruct((1, bs, hist, d), F32)),
        grid=(bs,),
        in_specs=[
            pl.BlockSpec((None, None, hist, d), lambda b: (0, b, 0, 0)),
            pl.BlockSpec((None, ts, d), lambda b: (b, 0, 0)),
            pl.BlockSpec((None, width, d), lambda b: (0, 0, 0)),
            pl.BlockSpec((1, d), lambda b: (0, 0)),
            pl.BlockSpec((1, d), lambda b: (0, 0)),
            pl.BlockSpec((1, d), lambda b: (0, 0)),
        ],
        out_specs=(pl.BlockSpec((None, ts, d), lambda b: (b, 0, 0)),
                   pl.BlockSpec((None, None, hist, d), lambda b: (0, b, 0, 0))),
        scratch_shapes=[pltpu.VMEM((hist + ts, d), F32)],
        compiler_params=_params(("parallel",)),
        name="conv_sample",
    )(state, u_s, w_dw, b_dw, ln_g, ln_b)


def _cumsum_body(x_ref, ct_ref, carry_ref, *, tc):
    @pl.when(pl.program_id(1) == 0)
    def _():
        carry_ref[...] = jnp.zeros_like(carry_ref)

    row = lax.broadcasted_iota(jnp.int32, (tc, tc), 0)
    col = lax.broadcasted_iota(jnp.int32, (tc, tc), 1)
    upper = _ones_where(row <= col)
    ctb = carry_ref[...]
    for piece in _split3(x_ref[...]):
        ctb = ctb + lax.dot_general(piece, upper, _TN, preferred_element_type=F32)
    for h in range(ct_ref.shape[0]):
        ct_ref[h] = ctb[h:h + 1, :]
    carry_ref[...] = ctb[:, tc - 1:tc]


def _cumsum_prompt(logf, n_seq, seq_len):
    nh = logf.shape[-1]
    tc = TC_CUMSUM
    nt = seq_len // tc
    return pl.pallas_call(
        functools.partial(_cumsum_body, tc=tc),
        out_shape=jax.ShapeDtypeStruct((n_seq, nh, 1, seq_len), F32),
        grid=(n_seq, nt),
        in_specs=[pl.BlockSpec((tc, nh), lambda b, j: (b * nt + j, 0))],
        out_specs=pl.BlockSpec((None, nh, 1, tc), lambda b, j: (b, 0, 0, j)),
        scratch_shapes=[pltpu.VMEM((nh, 1), F32)],
        compiler_params=_params(("parallel", "arbitrary")),
        name="cumsum_prompt",
    )(logf)


def _attn_prompt_body(q_ref, k_ref, v_ref, ctq_ref, ctk_ref, o_ref, m_ref, l_ref, acc_ref, cq_ref,
                      *, scale2):
    qi = pl.program_id(1)
    ki = pl.program_id(2)
    n_heads, tq, dh = q_ref.shape
    tk = k_ref.shape[1]
    reps = tk // dh

    @pl.when(ki == 0)
    def _():
        m_ref[...] = jnp.full_like(m_ref, NEG_INF)
        l_ref[...] = jnp.zeros_like(l_ref)
        acc_ref[...] = jnp.zeros_like(acc_ref)

        def init_head(h, carry):
            row = ctq_ref[h] * LOG2E
            cq_ref[h] = jnp.broadcast_to(row, (dh, tq)).T
            return carry

        lax.fori_loop(0, n_heads, init_head, 0)

    def run(diagonal):
        def head(h, carry):
            s = lax.dot_general(q_ref[h], k_ref[h], _NT, preferred_element_type=F32) * scale2
            s = s + _lane_tile(cq_ref[h], reps) - ctk_ref[h] * LOG2E
            if diagonal:
                row = lax.broadcasted_iota(jnp.int32, (tq, tk), 0)
                col = lax.broadcasted_iota(jnp.int32, (tq, tk), 1)
                s = jnp.where(col <= row, s, NEG_INF)
            m_prev = m_ref[h]
            m_new = jnp.maximum(m_prev, jnp.max(s, axis=-1, keepdims=True))
            alpha = jnp.exp2(m_prev - m_new)
            p = jnp.exp2(s - _lane_tile(m_new, reps))
            l_ref[h] = alpha * l_ref[h] + jnp.sum(p, axis=-1, keepdims=True)
            acc_ref[h] = alpha * acc_ref[h] + _dot(p.astype(BF16), v_ref[h])
            m_ref[h] = m_new
            return carry

        lax.fori_loop(0, n_heads, head, 0, unroll=2)

    @pl.when(ki < qi)
    def _():
        run(False)

    @pl.when(ki == qi)
    def _():
        run(True)

        def finish(h, carry):
            o_ref[h] = (acc_ref[h] / l_ref[h]).astype(o_ref.dtype)
            return carry

        lax.fori_loop(0, n_heads, finish, 0)


def _attn_prompt(q, k, v, ct, n_seq, seq_len):
    n_heads, _, dh = q.shape
    tq = TQ_ATTN
    nq = seq_len // tq
    stat = pltpu.VMEM((n_heads, tq, dh), F32)
    kv_map = lambda b, qi, ki: (0, b * nq + jnp.minimum(ki, qi), 0)
    return pl.pallas_call(
        functools.partial(_attn_prompt_body, scale2=dh ** -0.5 * LOG2E),
        out_shape=jax.ShapeDtypeStruct((n_heads, n_seq * seq_len, dh), BF16),
        grid=(n_seq, nq, nq),
        in_specs=[
            pl.BlockSpec((n_heads, tq, dh), lambda b, qi, ki: (0, b * nq + qi, 0)),
            pl.BlockSpec((n_heads, tq, dh), kv_map),
            pl.BlockSpec((n_heads, tq, dh), kv_map),
            pl.BlockSpec((None, n_heads, 1, tq), lambda b, qi, ki: (b, 0, 0, qi)),
            pl.BlockSpec((None, n_heads, 1, tq), lambda b, qi, ki: (b, 0, 0, jnp.minimum(ki, qi))),
        ],
        out_specs=pl.BlockSpec((n_heads, tq, dh), lambda b, qi, ki: (0, b * nq + qi, 0)),
        scratch_shapes=[stat, stat, stat, stat],
        compiler_params=_params(("parallel", "parallel", "arbitrary")),
        name="attn_prompt",
    )(q, k, v, ct, ct)


def _lane_iota(shape):
    return lax.broadcasted_iota(jnp.int32, shape, len(shape) - 1)


def _suffix_sums_page(x, n_heads):
    n_rows, n_lanes = x.shape
    lane = _lane_iota(x.shape)
    sub = lax.broadcasted_iota(jnp.int32, x.shape, 0)
    y = x
    sh = n_heads
    while sh < n_lanes:
        y = y + jnp.where(lane + sh < n_lanes, pltpu.roll(y, n_lanes - sh, 1), 0.0)
        sh *= 2
    z = jnp.where(lane < n_heads, y, 0.0)
    sh = n_heads
    while sh < n_lanes:
        z = z + pltpu.roll(z, sh, 1)
        sh *= 2
    zi = z
    sh = 1
    while sh < n_rows:
        zi = zi + jnp.where(sub + sh < n_rows, pltpu.roll(zi, n_rows - sh, 0), 0.0)
        sh *= 2
    return y + (zi - z), zi[0:1, :]


def _attn_sample_parts(q_ref, kn_ref, vn_ref, lfn_ref, *refs, n_heads, ts, scale2):
    npg = PAGES_PER_STEP
    k_refs = refs[0:npg]
    v_refs = refs[npg:2 * npg]
    lf_refs = refs[2 * npg:3 * npg]
    o_ref, mask_ref, m_ref, l_ref, acc_ref, tail_ref, cq_ref = refs[3 * npg:]
    rows, dh = q_ref.shape
    page, n_groups, hg, _ = k_refs[0].shape
    rg = rows // n_groups
    n_keys = page * hg

    def attend(s, v_bf16, state):
        rmax = jnp.max(s, axis=-1, keepdims=True)
        m_new = rmax if state is None else jnp.maximum(state[0], rmax)
        p = jnp.exp2(s - m_new)
        psum = jnp.sum(p, axis=-1, keepdims=True)
        pv = _dot(p.astype(BF16), v_bf16)
        if state is None:
            return m_new, psum, pv
        alpha = jnp.exp2(state[0] - m_new)
        return m_new, alpha * state[1] + psum, alpha * state[2] + pv

    def init():
        r_i = lax.broadcasted_iota(jnp.int32, (rg, n_keys), 0)
        l_i = _lane_iota((rg, n_keys))
        mask_ref[...] = jnp.where(l_i % hg == r_i // ts, 0.0, NEG_INF)

        n_new = ts * n_heads
        cn = jnp.broadcast_to(lfn_ref[...], (8, n_new))
        lane8 = _lane_iota((8, n_new))
        sh = n_heads
        while sh < n_new:
            cn = cn + jnp.where(lane8 >= sh, pltpu.roll(cn, sh, 1), 0.0)
            sh *= 2
        cn_row = cn[0:1, :]
        r2 = lax.broadcasted_iota(jnp.int32, (rows, n_new), 0)
        l2 = _lane_iota((rows, n_new))
        own = l2 == (r2 % ts) * n_heads + r2 // ts
        cq = jnp.sum(jnp.where(own, jnp.broadcast_to(cn_row, (rows, n_new)), 0.0),
                     axis=-1, keepdims=True)
        cq_ref[...] = cq * LOG2E
        tail_ref[...] = jnp.zeros_like(tail_ref)

        s = lax.dot_general(q_ref[...], kn_ref[...].astype(BF16), _NT,
                            preferred_element_type=F32) * scale2
        s = s + (cq - cn_row) * LOG2E
        valid = (l2 % n_heads == r2 // ts) & (l2 // n_heads <= r2 % ts)
        s = jnp.where(valid, s, NEG_INF)
        m, l, acc = attend(s, vn_ref[...].astype(BF16), None)
        m_ref[...] = m
        l_ref[...] = l
        acc_ref[...] = acc

    def pages():
        for g in range(n_groups):
            rs = pl.ds(g * rg, rg)
            tail = tail_ref[g, 0:1, :]
            e_tiles = [None] * npg
            for i in reversed(range(npg)):
                x = lf_refs[i][g]
                incl, tot = _suffix_sums_page(x, hg)
                e_tiles[i] = (incl - x + tail) * LOG2E
                tail = tail + tot
            tail_ref[g] = jnp.broadcast_to(tail, tail_ref.shape[1:])
            e_row = jnp.concatenate([e[a:a + 1, :] for e in e_tiles for a in range(e.shape[0])],
                                    axis=1)
            bias = jnp.concatenate([cq_ref[rs, :] + mask_ref[...]] * npg, axis=1) + e_row

            k_g = jnp.concatenate([r[:, g].reshape(n_keys, dh).astype(BF16) for r in k_refs], axis=0)
            v_g = jnp.concatenate([r[:, g].reshape(n_keys, dh).astype(BF16) for r in v_refs], axis=0)
            s = lax.dot_general(q_ref[rs, :], k_g, _NT, preferred_element_type=F32) * scale2
            state = attend(s + bias, v_g, (m_ref[rs, :], l_ref[rs, :], acc_ref[rs, :]))
            m_ref[rs, :], l_ref[rs, :], acc_ref[rs, :] = state

    def finish():
        o_ref[...] = acc_ref[...] / l_ref[...]

    return init, pages, finish


def _attn_sample_body(pt_ref, *refs, n_heads, ts, scale2):
    del pt_ref
    init, pages, finish = _attn_sample_parts(*refs, n_heads=n_heads, ts=ts, scale2=scale2)
    j = pl.program_id(1)
    pl.when(j == 0)(init)
    pages()
    pl.when(j == pl.num_programs(1) - 1)(finish)


def _attn_sample(q2, kn2, vn2, lfn, cache_k, cache_v, cache_logf, page_table, n_heads, ts):
    bs, rows, dh = q2.shape
    n_phys, page = cache_k.shape[0], cache_k.shape[1]
    npg = PAGES_PER_STEP
    n_steps = page_table.shape[1] // npg
    hg = SUBLANES
    n_groups = n_heads // hg
    n_keys = page * hg
    lanes = kn2.shape[1]
    ck = cache_k.reshape(n_phys, page, n_groups, hg, dh)
    cv = cache_v.reshape(n_phys, page, n_groups, hg, dh)
    lfc = cache_logf.reshape(n_phys, page, n_groups, hg).transpose(0, 2, 1, 3).reshape(
        n_phys, n_groups, n_keys // lanes, lanes)

    def page_map(p, nd):
        return lambda b, j, pt: (pt[b, (n_steps - 1 - j) * npg + p],) + (0,) * nd

    per_seq = lambda b, j, pt: (b, 0, 0)
    in_specs = [pl.BlockSpec((None, rows, dh), per_seq),
                pl.BlockSpec((None, lanes, dh), per_seq),
                pl.BlockSpec((None, lanes, dh), per_seq),
                pl.BlockSpec((None, 1, lanes), per_seq)]
    in_specs += [pl.BlockSpec((None, page, n_groups, hg, dh), page_map(p, 4)) for p in range(npg)]
    in_specs += [pl.BlockSpec((None, page, n_groups, hg, dh), page_map(p, 4)) for p in range(npg)]
    in_specs += [pl.BlockSpec((None, n_groups, n_keys // lanes, lanes), page_map(p, 3))
                 for p in range(npg)]
    grid_spec = pltpu.PrefetchScalarGridSpec(
        num_scalar_prefetch=1,
        grid=(bs, n_steps),
        in_specs=in_specs,
        out_specs=pl.BlockSpec((None, rows, dh), per_seq),
        scratch_shapes=[pltpu.VMEM((rows // n_groups, n_keys), F32), pltpu.VMEM((rows, 1), F32),
                        pltpu.VMEM((rows, 1), F32), pltpu.VMEM((rows, dh), F32),
                        pltpu.VMEM((n_groups, SUBLANES, lanes), F32), pltpu.VMEM((rows, 1), F32)],
    )
    return pl.pallas_call(
        functools.partial(_attn_sample_body, n_heads=n_heads, ts=ts, scale2=dh ** -0.5 * LOG2E),
        out_shape=jax.ShapeDtypeStruct((bs, rows, dh), F32),
        grid_spec=grid_spec,
        compiler_params=_params(("parallel", "arbitrary")),
        name="attn_sample",
    )(page_table, q2, kn2, vn2, lfn, *([ck] * npg), *([cv] * npg), *([lfc] * npg))


def _to_blocks(x):
    rows, c = x.shape
    return x.reshape(rows, c // LANES, LANES).transpose(1, 0, 2)


def kernel(x_prompt, x_sample, state_conv, cache_k, cache_v, cache_logf, page_table, norm_gain,
           ffn1_w_gate, ffn1_w_up, ffn1_w_down, ffn2_w_gate, ffn2_w_up, ffn2_w_down,
           conv_w_pw1, conv_b_pw1, conv_w_dw, conv_b_dw, conv_ln_g, conv_ln_b, conv_w_pw2,
           conv_b_pw2, kv_norm_g, w_k, w_v, w_fgate, b_fgate, attn_w_q, attn_w_o):
    n_seq, seq_len, d = x_prompt.shape
    bs, ts, _ = x_sample.shape
    depth = norm_gain.shape[0]
    n_heads = w_fgate.shape[-1]
    dh = w_k.shape[-1] // n_heads
    hist = conv_w_dw.shape[1] - 1
    assert depth == 2 and conv_w_pw1.shape[0] == 1 and attn_w_q.shape[0] == 1
    assert seq_len >= hist and ts * n_heads == LANES and dh == LANES
    mp, ms = n_seq * seq_len, bs * ts
    gains = norm_gain.reshape(depth * N_NORMS, 1, d)
    zero_bias = jnp.zeros((d,), F32)
    (ffn1_w_gate, ffn1_w_up, ffn1_w_down, ffn2_w_gate, ffn2_w_up, ffn2_w_down, conv_w_pw1,
     conv_w_pw2, w_k, w_v, attn_w_q, attn_w_o) = (
        w.astype(BF16) for w in (ffn1_w_gate, ffn1_w_up, ffn1_w_down, ffn2_w_gate, ffn2_w_up,
                                 ffn2_w_down, conv_w_pw1, conv_w_pw2, w_k, w_v, attn_w_q,
                                 attn_w_o))

    ffn_w = {
        0: (0, 1, ffn1_w_gate, ffn1_w_up, ffn1_w_down, 0),
        1: (4, 5, ffn2_w_gate, ffn2_w_up, ffn2_w_down, 0),
        2: (N_NORMS + 0, N_NORMS + 1, ffn1_w_gate, ffn1_w_up, ffn1_w_down, 1),
        3: (N_NORMS + 4, N_NORMS + 5, ffn2_w_gate, ffn2_w_up, ffn2_w_down, 1),
    }

    tm_s = ms
    hs = _ffn(x_sample.reshape(ms, d), gains, *ffn_w[0], tm=tm_s)
    u_s = _glu_proj(hs, gains, 2, conv_w_pw1, conv_b_pw1, tm=tm_s)
    y_s, new_conv_s = _conv_sample(u_s.reshape(bs, ts, d), state_conv, conv_w_dw, conv_b_dw,
                                   conv_ln_g, conv_ln_b)
    hs = _resid_mm(hs, _to_blocks(y_s.reshape(ms, d)).astype(BF16), conv_w_pw2, conv_b_pw2[0],
                   gains, 3, tm=tm_s)
    hs = _ffn(hs, gains, *ffn_w[1], tm=tm_s)
    k_s, v_s, lf_s = _kv_proj(hs, kv_norm_g, w_k, w_v, w_fgate, b_fgate, tm=tm_s, blocked=False)
    hs = _ffn(hs, gains, *ffn_w[2], tm=tm_s)
    q_s = _q_proj(hs, gains, N_NORMS + 2, attn_w_q, tm=tm_s)
    q2 = q_s.reshape(n_heads, bs, ts, dh).transpose(1, 0, 2, 3).reshape(bs, n_heads * ts, dh)
    o2 = _attn_sample(q2, k_s.reshape(bs, ts * n_heads, dh), v_s.reshape(bs, ts * n_heads, dh),
                      lf_s.reshape(bs, 1, ts * n_heads), cache_k, cache_v, cache_logf,
                      page_table, n_heads, ts)
    o_s = o2.reshape(bs, n_heads, ts, dh).transpose(1, 0, 2, 3).reshape(n_heads, ms, dh)
    hs = _resid_mm(hs, o_s.astype(BF16), attn_w_o, zero_bias, gains, N_NORMS + 3, tm=tm_s)
    hs = _ffn(hs, gains, *ffn_w[3], tm=tm_s)

    tm = TM_PROMPT
    hp = _ffn(x_prompt.reshape(mp, d), gains, *ffn_w[0], tm=tm)
    u_p = _glu_proj(hp, gains, 2, conv_w_pw1, conv_b_pw1, tm=tm)
    y_p = _conv_prompt(u_p, n_seq, seq_len, conv_w_dw, conv_b_dw, conv_ln_g, conv_ln_b)
    hp = _resid_mm(hp, y_p, conv_w_pw2, conv_b_pw2[0], gains, 3, tm=tm)
    hp = _ffn(hp, gains, *ffn_w[1], tm=tm)
    k_p, v_p, lf_p, kb_p, vb_p = _kv_proj(hp, kv_norm_g, w_k, w_v, w_fgate, b_fgate, tm=tm,
                                          blocked=True)
    hp = _ffn(hp, gains, *ffn_w[2], tm=tm)
    q_p = _q_proj(hp, gains, N_NORMS + 2, attn_w_q, tm=tm)
    ct = _cumsum_prompt(lf_p, n_seq, seq_len)
    o_p = _attn_prompt(q_p, kb_p, vb_p, ct, n_seq, seq_len)
    hp = _resid_mm(hp, o_p, attn_w_o, zero_bias, gains, N_NORMS + 3, tm=tm)
    hp = _ffn(hp, gains, *ffn_w[3], tm=tm)
    new_conv_p = u_p.reshape(n_seq, seq_len, d)[None, :, seq_len - hist:, :]

    return (hp.reshape(n_seq, seq_len, d), hs.reshape(bs, ts, d),
            new_conv_p,
            k_p.reshape(n_seq, seq_len, n_heads, dh), v_p.reshape(n_seq, seq_len, n_heads, dh),
            lf_p.reshape(n_seq, seq_len, n_heads),
            new_conv_s,
            k_s.reshape(bs, ts, n_heads, dh), v_s.reshape(bs, ts, n_heads, dh),
            lf_s.reshape(bs, ts, n_heads))
```

```python
import functools
import math

import jax
import jax.numpy as jnp
from jax import lax
from jax.experimental import pallas as pl
from jax.experimental.pallas import tpu as pltpu

F32 = jnp.float32
BF16 = jnp.bfloat16

RMS_EPS = 1e-6
LN_EPS = 1e-5
NEG_INF = -1e30
LOG2E = math.log2(math.e)
N_NORMS = 6
LANES = 128
SUBLANES = 8
PAGES_PER_STEP = 8

VMEM_LIMIT_BYTES = 56 * 1024 * 1024

TM_PROMPT = 1024
TF_FFN = 256
TN_PROJ = 512
TK_RESID = 512
TT_CONV = 256
TQ_ATTN = 512
TC_CUMSUM = 256

_NT = (((1,), (1,)), ((), ()))
_TN = (((0,), (0,)), ((), ()))


def _params(semantics):
    return pltpu.CompilerParams(dimension_semantics=semantics, vmem_limit_bytes=VMEM_LIMIT_BYTES)


def _rms(x, g):
    ms = jnp.mean(x * x, axis=-1, keepdims=True)
    return (x * lax.rsqrt(ms + RMS_EPS)) * g


def _dot(a, b):
    return jnp.dot(a, b, preferred_element_type=F32)


def _split3(x):
    hi = x.astype(BF16)
    r1 = x - hi.astype(F32)
    mid = r1.astype(BF16)
    lo = (r1 - mid.astype(F32)).astype(BF16)
    return hi, mid, lo


def _ones_where(mask):
    return jnp.where(mask, 1.0, 0.0).astype(BF16)


def _lane_tile(x, reps):
    return jnp.concatenate([x] * reps, axis=1)


def _ffn_body(x_ref, gpre_ref, gpost_ref, wg_ref, wu_ref, wd_ref, o_ref, xn_ref, *, d_ff, tf):
    f = pl.program_id(1)

    @pl.when(f == 0)
    def _():
        xn_ref[...] = _rms(x_ref[...], gpre_ref[...]).astype(BF16)
        o_ref[...] = jnp.zeros_like(o_ref)

    xn = xn_ref[...]
    gate = _dot(xn, wg_ref[...].astype(BF16))
    up = _dot(xn, wu_ref[...].astype(BF16))
    act = gate * jax.nn.sigmoid(gate) * up
    col = f * tf + lax.broadcasted_iota(jnp.int32, (1, tf), 1)
    act = jnp.where(col < d_ff, act, 0.0).astype(BF16)
    row = f * tf + lax.broadcasted_iota(jnp.int32, (tf, 1), 0)
    wd = jnp.where(row < d_ff, wd_ref[...], 0.0).astype(BF16)
    o_ref[...] += _dot(act, wd)

    @pl.when(f == pl.num_programs(1) - 1)
    def _():
        o_ref[...] = x_ref[...] + 0.5 * _rms(o_ref[...], gpost_ref[...])


def _ffn_specs(d, tm, tf, n_pre, n_post, layer):
    one = pl.Buffered(1)
    in_specs = [
        pl.BlockSpec((tm, d), lambda i, f, *_: (i, 0)),
        pl.BlockSpec((None, 1, d), lambda i, f, *_: (n_pre, 0, 0)),
        pl.BlockSpec((None, 1, d), lambda i, f, *_: (n_post, 0, 0)),
        pl.BlockSpec((None, d, tf), lambda i, f, *_: (layer, 0, f)),
        pl.BlockSpec((None, d, tf), lambda i, f, *_: (layer, 0, f)),
        pl.BlockSpec((None, tf, d), lambda i, f, *_: (layer, f, 0)),
    ]
    return in_specs, pl.BlockSpec((tm, d), lambda i, f, *_: (i, 0), pipeline_mode=one)


def _ffn(h, gains, n_pre, n_post, w_gate, w_up, w_down, layer, *, tm):
    m, d = h.shape
    d_ff = w_gate.shape[-1]
    tf = TF_FFN
    in_specs, out_spec = _ffn_specs(d, tm, tf, n_pre, n_post, layer)
    return pl.pallas_call(
        functools.partial(_ffn_body, d_ff=d_ff, tf=tf),
        out_shape=jax.ShapeDtypeStruct((m, d), F32),
        grid=(m // tm, pl.cdiv(d_ff, tf)),
        in_specs=in_specs,
        out_specs=out_spec,
        scratch_shapes=[pltpu.VMEM((tm, d), BF16)],
        compiler_params=_params(("parallel", "arbitrary")),
        name="ffn",
    )(h, gains, gains, w_gate, w_up, w_down)


def _glu_body(x_ref, g_ref, wa_ref, wg_ref, ba_ref, bg_ref, o_ref, xn_ref):
    @pl.when(pl.program_id(1) == 0)
    def _():
        xn_ref[...] = _rms(x_ref[...], g_ref[...]).astype(BF16)

    xn = xn_ref[...]
    a = _dot(xn, wa_ref[...].astype(BF16)) + ba_ref[...]
    gate = _dot(xn, wg_ref[...].astype(BF16)) + bg_ref[...]
    o_ref[...] = a * jax.nn.sigmoid(gate)


def _glu_proj(h, gains, n_gain, w_pw1, b_pw1, *, tm):
    m, d = h.shape
    dc = w_pw1.shape[-1] // 2
    tn = TN_PROJ
    nj = dc // tn
    b3 = b_pw1.reshape(b_pw1.shape[0], 1, 2 * dc)
    return pl.pallas_call(
        _glu_body,
        out_shape=jax.ShapeDtypeStruct((m, dc), F32),
        grid=(m // tm, nj),
        in_specs=[
            pl.BlockSpec((tm, d), lambda i, j: (i, 0)),
            pl.BlockSpec((None, 1, d), lambda i, j: (n_gain, 0, 0)),
            pl.BlockSpec((None, d, tn), lambda i, j: (0, 0, j)),
            pl.BlockSpec((None, d, tn), lambda i, j: (0, 0, j + nj)),
            pl.BlockSpec((None, 1, tn), lambda i, j: (0, 0, j)),
            pl.BlockSpec((None, 1, tn), lambda i, j: (0, 0, j + nj)),
        ],
        out_specs=pl.BlockSpec((tm, tn), lambda i, j: (i, j)),
        scratch_shapes=[pltpu.VMEM((tm, d), BF16)],
        compiler_params=_params(("parallel", "arbitrary")),
        name="glu_proj",
    )(h, gains, w_pw1, w_pw1, b3, b3)


def _kv_body(x_ref, g_ref, wk_ref, wv_ref, wf_ref, bf_ref, k_ref, v_ref, lf_ref, *rest, blocked):
    xn_ref = rest[-1]

    @pl.when(pl.program_id(1) == 0)
    def _():
        xn = _rms(x_ref[...], g_ref[...]).astype(BF16)
        xn_ref[...] = xn
        z = _dot(xn, wf_ref[...].astype(BF16)) + bf_ref[...]
        lf_ref[...] = jnp.minimum(z, 0.0) - jnp.log1p(jnp.exp(-jnp.abs(z)))

    xn = xn_ref[...]
    k = _dot(xn, wk_ref[...].astype(BF16))
    v = _dot(xn, wv_ref[...].astype(BF16))
    k_ref[...] = k
    v_ref[...] = v
    if blocked:
        kb_ref, vb_ref = rest[0], rest[1]
        for c in range(kb_ref.shape[0]):
            cs = slice(c * LANES, (c + 1) * LANES)
            kb_ref[c] = k[:, cs].astype(BF16)
            vb_ref[c] = v[:, cs].astype(BF16)


def _kv_proj(h, g, w_k, w_v, w_f, b_f, *, tm, blocked):
    m, d = h.shape
    da = w_k.shape[-1]
    nh = w_f.shape[-1]
    tn = TN_PROJ
    nc = tn // LANES
    out_shape = [jax.ShapeDtypeStruct((m, da), F32), jax.ShapeDtypeStruct((m, da), F32),
                 jax.ShapeDtypeStruct((m, nh), F32)]
    out_specs = [pl.BlockSpec((tm, tn), lambda i, j: (i, j)),
                 pl.BlockSpec((tm, tn), lambda i, j: (i, j)),
                 pl.BlockSpec((tm, nh), lambda i, j: (i, 0))]
    if blocked:
        out_shape += [jax.ShapeDtypeStruct((da // LANES, m, LANES), BF16)] * 2
        out_specs += [pl.BlockSpec((nc, tm, LANES), lambda i, j: (j, i, 0))] * 2
    return pl.pallas_call(
        functools.partial(_kv_body, blocked=blocked),
        out_shape=tuple(out_shape),
        grid=(m // tm, da // tn),
        in_specs=[
            pl.BlockSpec((tm, d), lambda i, j: (i, 0)),
            pl.BlockSpec((1, d), lambda i, j: (0, 0)),
            pl.BlockSpec((d, tn), lambda i, j: (0, j)),
            pl.BlockSpec((d, tn), lambda i, j: (0, j)),
            pl.BlockSpec((d, nh), lambda i, j: (0, 0)),
            pl.BlockSpec((1, nh), lambda i, j: (0, 0)),
        ],
        out_specs=tuple(out_specs),
        scratch_shapes=[pltpu.VMEM((tm, d), BF16)],
        compiler_params=_params(("parallel", "arbitrary")),
        name="kv_proj",
    )(h, g.reshape(1, d), w_k, w_v, w_f, b_f.reshape(1, nh))


def _q_body(x_ref, g_ref, w_ref, o_ref, xn_ref):
    @pl.when(pl.program_id(1) == 0)
    def _():
        xn_ref[...] = _rms(x_ref[...], g_ref[...]).astype(BF16)

    q = _dot(xn_ref[...], w_ref[...].astype(BF16))
    for c in range(o_ref.shape[0]):
        o_ref[c] = q[:, c * LANES:(c + 1) * LANES].astype(o_ref.dtype)


def _q_proj(h, gains, n_gain, w_q, *, tm):
    m, d = h.shape
    da = w_q.shape[-1]
    tn = TN_PROJ
    nc = tn // LANES
    return pl.pallas_call(
        _q_body,
        out_shape=jax.ShapeDtypeStruct((da // LANES, m, LANES), BF16),
        grid=(m // tm, da // tn),
        in_specs=[
            pl.BlockSpec((tm, d), lambda i, j: (i, 0)),
            pl.BlockSpec((None, 1, d), lambda i, j: (n_gain, 0, 0)),
            pl.BlockSpec((None, d, tn), lambda i, j: (0, 0, j)),
        ],
        out_specs=pl.BlockSpec((nc, tm, LANES), lambda i, j: (j, i, 0)),
        scratch_shapes=[pltpu.VMEM((tm, d), BF16)],
        compiler_params=_params(("parallel", "arbitrary")),
        name="q_proj",
    )(h, gains, w_q)


def _resid_mm_body(h_ref, y_ref, w_ref, b_ref, g_ref, o_ref):
    k = pl.program_id(1)

    @pl.when(k == 0)
    def _():
        o_ref[...] = jnp.broadcast_to(b_ref[...], o_ref.shape)

    y = jnp.concatenate([y_ref[c] for c in range(y_ref.shape[0])], axis=1)
    o_ref[...] += _dot(y, w_ref[...].astype(BF16))

    @pl.when(k == pl.num_programs(1) - 1)
    def _():
        o_ref[...] = h_ref[...] + _rms(o_ref[...], g_ref[...])


def _resid_mm(h, y, w, b, gains, n_gain, *, tm):
    m, d = h.shape
    tk = TK_RESID
    nc = tk // LANES
    return pl.pallas_call(
        _resid_mm_body,
        out_shape=jax.ShapeDtypeStruct((m, d), F32),
        grid=(m // tm, y.shape[0] // nc),
        in_specs=[
            pl.BlockSpec((tm, d), lambda i, k: (i, 0)),
            pl.BlockSpec((nc, tm, LANES), lambda i, k: (k, i, 0)),
            pl.BlockSpec((None, tk, d), lambda i, k: (0, k, 0)),
            pl.BlockSpec((1, d), lambda i, k: (0, 0)),
            pl.BlockSpec((None, 1, d), lambda i, k: (n_gain, 0, 0)),
        ],
        out_specs=pl.BlockSpec((tm, d), lambda i, k: (i, 0)),
        compiler_params=_params(("parallel", "arbitrary")),
        name="resid_mm",
    )(h, y, w, b.reshape(1, d), gains)


def _ln_silu(y, g, b):
    mu = jnp.mean(y, axis=-1, keepdims=True)
    dlt = y - mu
    var = jnp.mean(dlt * dlt, axis=-1, keepdims=True)
    z = (dlt * lax.rsqrt(var + LN_EPS)) * g + b
    return z * jax.nn.sigmoid(z)


_CONV_COLS = 512
_HALO = 32


def _conv_prompt_body(halo_ref, u_ref, wdw_ref, bdw_ref, lng_ref, lnb_ref, o_ref, ubuf_ref, ybuf_ref,
                      wb_ref, *, tt, width):
    i = pl.program_id(1)
    d = u_ref.shape[-1]
    ubuf_ref[pl.ds(0, _HALO), :] = jnp.where(i > 0, halo_ref[...], 0.0)
    ubuf_ref[pl.ds(_HALO, tt), :] = u_ref[...]
    for w in range(width):
        wb_ref[w] = jnp.broadcast_to(wdw_ref[w:w + 1, :], (SUBLANES, d))
    off = _HALO - (width - 1)
    n_tiles = (off + width - 1 + SUBLANES - 1) // SUBLANES + 1

    def conv_chunk(r, carry):
        base = pl.multiple_of(r * SUBLANES, SUBLANES)
        sub = lax.broadcasted_iota(jnp.int32, (SUBLANES, _CONV_COLS), 0)
        for cb in range(d // _CONV_COLS):
            cs = slice(cb * _CONV_COLS, (cb + 1) * _CONV_COLS)
            tiles = [ubuf_ref[pl.ds(base + SUBLANES * a, SUBLANES), cs] for a in range(n_tiles)]
            acc = jnp.broadcast_to(bdw_ref[:, cs], (SUBLANES, _CONV_COLS))
            for s in range(SUBLANES):
                z = None
                for a in range(n_tiles):
                    w = SUBLANES * a + s - off
                    if not 0 <= w < width:
                        continue
                    src = tiles[a] if s == 0 else jnp.where(sub >= s, tiles[a], tiles[a + 1])
                    term = src * wb_ref[w, :, cs]
                    z = term if z is None else z + term
                acc = acc + (z if s == 0 else pltpu.roll(z, SUBLANES - s, 0))
            ybuf_ref[pl.ds(base, SUBLANES), cs] = acc
        return carry

    lax.fori_loop(0, tt // SUBLANES, conv_chunk, 0)

    def ln_chunk(r, carry):
        base = pl.multiple_of(r * 16, 16)
        y = _ln_silu(ybuf_ref[pl.ds(base, 16), :], lng_ref[...], lnb_ref[...]).astype(o_ref.dtype)
        for c in range(o_ref.shape[0]):
            o_ref[c, pl.ds(base, 16), :] = y[:, c * LANES:(c + 1) * LANES]
        return carry

    lax.fori_loop(0, tt // 16, ln_chunk, 0, unroll=2)


def _conv_prompt(u, n_seq, seq_len, w_dw, b_dw, ln_g, ln_b):
    d = u.shape[-1]
    width = w_dw.shape[1]
    tt = TT_CONV
    nt = seq_len // tt
    hb = tt // _HALO
    return pl.pallas_call(
        functools.partial(_conv_prompt_body, tt=tt, width=width),
        out_shape=jax.ShapeDtypeStruct((d // LANES, n_seq * seq_len, LANES), BF16),
        grid=(n_seq, nt),
        in_specs=[
            pl.BlockSpec((_HALO, d), lambda b, i: (jnp.maximum((b * nt + i) * hb - 1, 0), 0)),
            pl.BlockSpec((tt, d), lambda b, i: (b * nt + i, 0)),
            pl.BlockSpec((None, width, d), lambda b, i: (0, 0, 0)),
            pl.BlockSpec((1, d), lambda b, i: (0, 0)),
            pl.BlockSpec((1, d), lambda b, i: (0, 0)),
            pl.BlockSpec((1, d), lambda b, i: (0, 0)),
        ],
        out_specs=pl.BlockSpec((d // LANES, tt, LANES), lambda b, i: (0, b * nt + i, 0)),
        scratch_shapes=[pltpu.VMEM((tt + _HALO, d), F32), pltpu.VMEM((tt, d), F32),
                        pltpu.VMEM((width, SUBLANES, d), F32)],
        compiler_params=_params(("parallel", "parallel")),
        name="conv_prompt",
    )(u, u, w_dw, b_dw, ln_g, ln_b)


def _conv_sample_body(st_ref, u_ref, wdw_ref, bdw_ref, lng_ref, lnb_ref, y_ref, ns_ref, ubuf_ref,
                      *, ts, width):
    hist = width - 1
    ubuf_ref[pl.ds(0, hist), :] = st_ref[...]
    ubuf_ref[pl.ds(hist, ts), :] = u_ref[...]
    full = ubuf_ref[...]
    acc = jnp.broadcast_to(bdw_ref[...], (ts, full.shape[-1]))
    for w in range(width):
        acc = acc + full[w:w + ts] * wdw_ref[w:w + 1, :]
    y_ref[...] = _ln_silu(acc, lng_ref[...], lnb_ref[...])
    ns_ref[...] = full[ts:ts + hist]


def _conv_sample(u_s, state, w_dw, b_dw, ln_g, ln_b):
    bs, ts, d = u_s.shape
    width = w_dw.shape[1]
    hist = width - 1
    return pl.pallas_call(
        functools.partial(_conv_sample_body, ts=ts, width=width),
        out_shape=(jax.ShapeDtypeStruct((bs, ts, d), F32),
                   jax.ShapeDtypeStruct((1, bs, hist, d), F32)),
        grid=(bs,),
        in_specs=[
            pl.BlockSpec((None, None, hist, d), lambda b: (0, b, 0, 0)),
            pl.BlockSpec((None, ts, d), lambda b: (b, 0, 0)),
            pl.BlockSpec((None, width, d), lambda b: (0, 0, 0)),
            pl.BlockSpec((1, d), lambda b: (0, 0)),
            pl.BlockSpec((1, d), lambda b: (0, 0)),
            pl.BlockSpec((1, d), lambda b: (0, 0)),
        ],
        out_specs=(pl.BlockSpec((None, ts, d), lambda b: (b, 0, 0)),
                   pl.BlockSpec((None, None, hist, d), lambda b: (0, b, 0, 0))),
        scratch_shapes=[pltpu.VMEM((hist + ts, d), F32)],
        compiler_params=_params(("parallel",)),
        name="conv_sample",
    )(state, u_s, w_dw, b_dw, ln_g, ln_b)


def _cumsum_body(x_ref, ct_ref, carry_ref, *, tc):
    @pl.when(pl.program_id(1) == 0)
    def _():
        carry_ref[...] = jnp.zeros_like(carry_ref)

    row = lax.broadcasted_iota(jnp.int32, (tc, tc), 0)
    col = lax.broadcasted_iota(jnp.int32, (tc, tc), 1)
    upper = _ones_where(row <= col)
    ctb = carry_ref[...]
    for piece in _split3(x_ref[...]):
        ctb = ctb + lax.dot_general(piece, upper, _TN, preferred_element_type=F32)
    for h in range(ct_ref.shape[0]):
        ct_ref[h] = ctb[h:h + 1, :]
    carry_ref[...] = ctb[:, tc - 1:tc]


def _cumsum_prompt(logf, n_seq, seq_len):
    nh = logf.shape[-1]
    tc = TC_CUMSUM
    nt = seq_len // tc
    return pl.pallas_call(
        functools.partial(_cumsum_body, tc=tc),
        out_shape=jax.ShapeDtypeStruct((n_seq, nh, 1, seq_len), F32),
        grid=(n_seq, nt),
        in_specs=[pl.BlockSpec((tc, nh), lambda b, j: (b * nt + j, 0))],
        out_specs=pl.BlockSpec((None, nh, 1, tc), lambda b, j: (b, 0, 0, j)),
        scratch_shapes=[pltpu.VMEM((nh, 1), F32)],
        compiler_params=_params(("parallel", "arbitrary")),
        name="cumsum_prompt",
    )(logf)


def _attn_prompt_body(q_ref, k_ref, v_ref, ctq_ref, ctk_ref, o_ref, m_ref, l_ref, acc_ref, cq_ref,
                      *, scale2):
    qi = pl.program_id(1)
    ki = pl.program_id(2)
    n_heads, tq, dh = q_ref.shape
    tk = k_ref.shape[1]
    reps = tk // dh

    @pl.when(ki == 0)
    def _():
        m_ref[...] = jnp.full_like(m_ref, NEG_INF)
        l_ref[...] = jnp.zeros_like(l_ref)
        acc_ref[...] = jnp.zeros_like(acc_ref)

        def init_head(h, carry):
            row = ctq_ref[h] * LOG2E
            cq_ref[h] = jnp.broadcast_to(row, (dh, tq)).T
            return carry

        lax.fori_loop(0, n_heads, init_head, 0)

    def run(diagonal):
        def head(h, carry):
            s = lax.dot_general(q_ref[h], k_ref[h], _NT, preferred_element_type=F32) * scale2
            s = s + _lane_tile(cq_ref[h], reps) - ctk_ref[h] * LOG2E
            if diagonal:
                row = lax.broadcasted_iota(jnp.int32, (tq, tk), 0)
                col = lax.broadcasted_iota(jnp.int32, (tq, tk), 1)
                s = jnp.where(col <= row, s, NEG_INF)
            m_prev = m_ref[h]
            m_new = jnp.maximum(m_prev, jnp.max(s, axis=-1, keepdims=True))
            alpha = jnp.exp2(m_prev - m_new)
            p = jnp.exp2(s - _lane_tile(m_new, reps))
            l_ref[h] = alpha * l_ref[h] + jnp.sum(p, axis=-1, keepdims=True)
            acc_ref[h] = alpha * acc_ref[h] + _dot(p.astype(BF16), v_ref[h])
            m_ref[h] = m_new
            return carry

        lax.fori_loop(0, n_heads, head, 0, unroll=2)

    @pl.when(ki < qi)
    def _():
        run(False)

    @pl.when(ki == qi)
    def _():
        run(True)

        def finish(h, carry):
            o_ref[h] = (acc_ref[h] / l_ref[h]).astype(o_ref.dtype)
            return carry

        lax.fori_loop(0, n_heads, finish, 0)


def _attn_prompt(q, k, v, ct, n_seq, seq_len):
    n_heads, _, dh = q.shape
    tq = TQ_ATTN
    nq = seq_len // tq
    stat = pltpu.VMEM((n_heads, tq, dh), F32)
    kv_map = lambda b, qi, ki: (0, b * nq + jnp.minimum(ki, qi), 0)
    return pl.pallas_call(
        functools.partial(_attn_prompt_body, scale2=dh ** -0.5 * LOG2E),
        out_shape=jax.ShapeDtypeStruct((n_heads, n_seq * seq_len, dh), BF16),
        grid=(n_seq, nq, nq),
        in_specs=[
            pl.BlockSpec((n_heads, tq, dh), lambda b, qi, ki: (0, b * nq + qi, 0)),
            pl.BlockSpec((n_heads, tq, dh), kv_map),
            pl.BlockSpec((n_heads, tq, dh), kv_map),
            pl.BlockSpec((None, n_heads, 1, tq), lambda b, qi, ki: (b, 0, 0, qi)),
            pl.BlockSpec((None, n_heads, 1, tq), lambda b, qi, ki: (b, 0, 0, jnp.minimum(ki, qi))),
        ],
        out_specs=pl.BlockSpec((n_heads, tq, dh), lambda b, qi, ki: (0, b * nq + qi, 0)),
        scratch_shapes=[stat, stat, stat, stat],
        compiler_params=_params(("parallel", "parallel", "arbitrary")),
        name="attn_prompt",
    )(q, k, v, ct, ct)


def _lane_iota(shape):
    return lax.broadcasted_iota(jnp.int32, shape, len(shape) - 1)


def _suffix_sums_page(x, n_heads):
    n_rows, n_lanes = x.shape
    lane = _lane_iota(x.shape)
    sub = lax.broadcasted_iota(jnp.int32, x.shape, 0)
    y = x
    sh = n_heads
    while sh < n_lanes:
        y = y + jnp.where(lane + sh < n_lanes, pltpu.roll(y, n_lanes - sh, 1), 0.0)
        sh *= 2
    z = jnp.where(lane < n_heads, y, 0.0)
    sh = n_heads
    while sh < n_lanes:
        z = z + pltpu.roll(z, sh, 1)
        sh *= 2
    zi = z
    sh = 1
    while sh < n_rows:
        zi = zi + jnp.where(sub + sh < n_rows, pltpu.roll(zi, n_rows - sh, 0), 0.0)
        sh *= 2
    return y + (zi - z), zi[0:1, :]


def _attn_sample_parts(q_ref, kn_ref, vn_ref, lfn_ref, *refs, n_heads, ts, scale2):
    npg = PAGES_PER_STEP
    k_refs = refs[0:npg]
    v_refs = refs[npg:2 * npg]
    lf_refs = refs[2 * npg:3 * npg]
    o_ref, mask_ref, m_ref, l_ref, acc_ref, tail_ref, cq_ref = refs[3 * npg:]
    rows, dh = q_ref.shape
    page, n_groups, hg, _ = k_refs[0].shape
    rg = rows // n_groups
    n_keys = page * hg

    def attend(s, v_bf16, state):
        rmax = jnp.max(s, axis=-1, keepdims=True)
        m_new = rmax if state is None else jnp.maximum(state[0], rmax)
        p = jnp.exp2(s - m_new)
        psum = jnp.sum(p, axis=-1, keepdims=True)
        pv = _dot(p.astype(BF16), v_bf16)
        if state is None:
            return m_new, psum, pv
        alpha = jnp.exp2(state[0] - m_new)
        return m_new, alpha * state[1] + psum, alpha * state[2] + pv

    def init():
        r_i = lax.broadcasted_iota(jnp.int32, (rg, n_keys), 0)
        l_i = _lane_iota((rg, n_keys))
        mask_ref[...] = jnp.where(l_i % hg == r_i // ts, 0.0, NEG_INF)

        n_new = ts * n_heads
        cn = jnp.broadcast_to(lfn_ref[...], (8, n_new))
        lane8 = _lane_iota((8, n_new))
        sh = n_heads
        while sh < n_new:
            cn = cn + jnp.where(lane8 >= sh, pltpu.roll(cn, sh, 1), 0.0)
            sh *= 2
        cn_row = cn[0:1, :]
        r2 = lax.broadcasted_iota(jnp.int32, (rows, n_new), 0)
        l2 = _lane_iota((rows, n_new))
        own = l2 == (r2 % ts) * n_heads + r2 // ts
        cq = jnp.sum(jnp.where(own, jnp.broadcast_to(cn_row, (rows, n_new)), 0.0),
                     axis=-1, keepdims=True)
        cq_ref[...] = cq * LOG2E
        tail_ref[...] = jnp.zeros_like(tail_ref)

        s = lax.dot_general(q_ref[...], kn_ref[...].astype(BF16), _NT,
                            preferred_element_type=F32) * scale2
        s = s + (cq - cn_row) * LOG2E
        valid = (l2 % n_heads == r2 // ts) & (l2 // n_heads <= r2 % ts)
        s = jnp.where(valid, s, NEG_INF)
        m, l, acc = attend(s, vn_ref[...].astype(BF16), None)
        m_ref[...] = m
        l_ref[...] = l
        acc_ref[...] = acc

    def pages():
        for g in range(n_groups):
            rs = pl.ds(g * rg, rg)
            tail = tail_ref[g, 0:1, :]
            e_tiles = [None] * npg
            for i in reversed(range(npg)):
                x = lf_refs[i][g]
                incl, tot = _suffix_sums_page(x, hg)
                e_tiles[i] = (incl - x + tail) * LOG2E
                tail = tail + tot
            tail_ref[g] = jnp.broadcast_to(tail, tail_ref.shape[1:])
            e_row = jnp.concatenate([e[a:a + 1, :] for e in e_tiles for a in range(e.shape[0])],
                                    axis=1)
            bias = jnp.concatenate([cq_ref[rs, :] + mask_ref[...]] * npg, axis=1) + e_row

            k_g = jnp.concatenate([r[:, g].reshape(n_keys, dh).astype(BF16) for r in k_refs], axis=0)
            v_g = jnp.concatenate([r[:, g].reshape(n_keys, dh).astype(BF16) for r in v_refs], axis=0)
            s = lax.dot_general(q_ref[rs, :], k_g, _NT, preferred_element_type=F32) * scale2
            state = attend(s + bias, v_g, (m_ref[rs, :], l_ref[rs, :], acc_ref[rs, :]))
            m_ref[rs, :], l_ref[rs, :], acc_ref[rs, :] = state

    def finish():
        o_ref[...] = acc_ref[...] / l_ref[...]

    return init, pages, finish


def _attn_sample_body(pt_ref, *refs, n_heads, ts, scale2):
    del pt_ref
    init, pages, finish = _attn_sample_parts(*refs, n_heads=n_heads, ts=ts, scale2=scale2)
    j = pl.program_id(1)
    pl.when(j == 0)(init)
    pages()
    pl.when(j == pl.num_programs(1) - 1)(finish)


def _attn_sample(q2, kn2, vn2, lfn, cache_k, cache_v, cache_logf, page_table, n_heads, ts):
    bs, rows, dh = q2.shape
    n_phys, page = cache_k.shape[0], cache_k.shape[1]
    npg = PAGES_PER_STEP
    n_steps = page_table.shape[1] // npg
    hg = SUBLANES
    n_groups = n_heads // hg
    n_keys = page * hg
    lanes = kn2.shape[1]
    ck = cache_k.reshape(n_phys, page, n_groups, hg, dh)
    cv = cache_v.reshape(n_phys, page, n_groups, hg, dh)
    lfc = cache_logf.reshape(n_phys, page, n_groups, hg).transpose(0, 2, 1, 3).reshape(
        n_phys, n_groups, n_keys // lanes, lanes)

    def page_map(p, nd):
        return lambda b, j, pt: (pt[b, (n_steps - 1 - j) * npg + p],) + (0,) * nd

    per_seq = lambda b, j, pt: (b, 0, 0)
    in_specs = [pl.BlockSpec((None, rows, dh), per_seq),
                pl.BlockSpec((None, lanes, dh), per_seq),
                pl.BlockSpec((None, lanes, dh), per_seq),
                pl.BlockSpec((None, 1, lanes), per_seq)]
    in_specs += [pl.BlockSpec((None, page, n_groups, hg, dh), page_map(p, 4)) for p in range(npg)]
    in_specs += [pl.BlockSpec((None, page, n_groups, hg, dh), page_map(p, 4)) for p in range(npg)]
    in_specs += [pl.BlockSpec((None, n_groups, n_keys // lanes, lanes), page_map(p, 3))
                 for p in range(npg)]
    grid_spec = pltpu.PrefetchScalarGridSpec(
        num_scalar_prefetch=1,
        grid=(bs, n_steps),
        in_specs=in_specs,
        out_specs=pl.BlockSpec((None, rows, dh), per_seq),
        scratch_shapes=[pltpu.VMEM((rows // n_groups, n_keys), F32), pltpu.VMEM((rows, 1), F32),
                        pltpu.VMEM((rows, 1), F32), pltpu.VMEM((rows, dh), F32),
                        pltpu.VMEM((n_groups, SUBLANES, lanes), F32), pltpu.VMEM((rows, 1), F32)],
    )
    return pl.pallas_call(
        functools.partial(_attn_sample_body, n_heads=n_heads, ts=ts, scale2=dh ** -0.5 * LOG2E),
        out_shape=jax.ShapeDtypeStruct((bs, rows, dh), F32),
        grid_spec=grid_spec,
        compiler_params=_params(("parallel", "arbitrary")),
        name="attn_sample",
    )(page_table, q2, kn2, vn2, lfn, *([ck] * npg), *([cv] * npg), *([lfc] * npg))


def _to_blocks(x):
    rows, c = x.shape
    return x.reshape(rows, c // LANES, LANES).transpose(1, 0, 2)


def kernel(x_prompt, x_sample, state_conv, cache_k, cache_v, cache_logf, page_table, norm_gain,
           ffn1_w_gate, ffn1_w_up, ffn1_w_down, ffn2_w_gate, ffn2_w_up, ffn2_w_down,
           conv_w_pw1, conv_b_pw1, conv_w_dw, conv_b_dw, conv_ln_g, conv_ln_b, conv_w_pw2,
           conv_b_pw2, kv_norm_g, w_k, w_v, w_fgate, b_fgate, attn_w_q, attn_w_o):
    n_seq, seq_len, d = x_prompt.shape
    bs, ts, _ = x_sample.shape
    depth = norm_gain.shape[0]
    n_heads = w_fgate.shape[-1]
    dh = w_k.shape[-1] // n_heads
    hist = conv_w_dw.shape[1] - 1
    assert depth == 2 and conv_w_pw1.shape[0] == 1 and attn_w_q.shape[0] == 1
    assert seq_len >= hist and ts * n_heads == LANES and dh == LANES
    mp, ms = n_seq * seq_len, bs * ts
    gains = norm_gain.reshape(depth * N_NORMS, 1, d)
    zero_bias = jnp.zeros((d,), F32)

    ffn_w = {
        0: (0, 1, ffn1_w_gate, ffn1_w_up, ffn1_w_down, 0),
        1: (4, 5, ffn2_w_gate, ffn2_w_up, ffn2_w_down, 0),
        2: (N_NORMS + 0, N_NORMS + 1, ffn1_w_gate, ffn1_w_up, ffn1_w_down, 1),
        3: (N_NORMS + 4, N_NORMS + 5, ffn2_w_gate, ffn2_w_up, ffn2_w_down, 1),
    }

    tm_s = ms
    hs = _ffn(x_sample.reshape(ms, d), gains, *ffn_w[0], tm=tm_s)
    u_s = _glu_proj(hs, gains, 2, conv_w_pw1, conv_b_pw1, tm=tm_s)
    y_s, new_conv_s = _conv_sample(u_s.reshape(bs, ts, d), state_conv, conv_w_dw, conv_b_dw,
                                   conv_ln_g, conv_ln_b)
    hs = _resid_mm(hs, _to_blocks(y_s.reshape(ms, d)).astype(BF16), conv_w_pw2, conv_b_pw2[0],
                   gains, 3, tm=tm_s)
    hs = _ffn(hs, gains, *ffn_w[1], tm=tm_s)
    k_s, v_s, lf_s = _kv_proj(hs, kv_norm_g, w_k, w_v, w_fgate, b_fgate, tm=tm_s, blocked=False)
    hs = _ffn(hs, gains, *ffn_w[2], tm=tm_s)
    q_s = _q_proj(hs, gains, N_NORMS + 2, attn_w_q, tm=tm_s)
    q2 = q_s.reshape(n_heads, bs, ts, dh).transpose(1, 0, 2, 3).reshape(bs, n_heads * ts, dh)
    o2 = _attn_sample(q2, k_s.reshape(bs, ts * n_heads, dh), v_s.reshape(bs, ts * n_heads, dh),
                      lf_s.reshape(bs, 1, ts * n_heads), cache_k, cache_v, cache_logf,
                      page_table, n_heads, ts)
    o_s = o2.reshape(bs, n_heads, ts, dh).transpose(1, 0, 2, 3).reshape(n_heads, ms, dh)
    hs = _resid_mm(hs, o_s.astype(BF16), attn_w_o, zero_bias, gains, N_NORMS + 3, tm=tm_s)
    hs = _ffn(hs, gains, *ffn_w[3], tm=tm_s)

    tm = TM_PROMPT
    hp = _ffn(x_prompt.reshape(mp, d), gains, *ffn_w[0], tm=tm)
    u_p = _glu_proj(hp, gains, 2, conv_w_pw1, conv_b_pw1, tm=tm)
    y_p = _conv_prompt(u_p, n_seq, seq_len, conv_w_dw, conv_b_dw, conv_ln_g, conv_ln_b)
    hp = _resid_mm(hp, y_p, conv_w_pw2, conv_b_pw2[0], gains, 3, tm=tm)
    hp = _ffn(hp, gains, *ffn_w[1], tm=tm)
    k_p, v_p, lf_p, kb_p, vb_p = _kv_proj(hp, kv_norm_g, w_k, w_v, w_fgate, b_fgate, tm=tm,
                                          blocked=True)
    hp = _ffn(hp, gains, *ffn_w[2], tm=tm)
    q_p = _q_proj(hp, gains, N_NORMS + 2, attn_w_q, tm=tm)
    ct = _cumsum_prompt(lf_p, n_seq, seq_len)
    o_p = _attn_prompt(q_p, kb_p, vb_p, ct, n_seq, seq_len)
    hp = _resid_mm(hp, o_p, attn_w_o, zero_bias, gains, N_NORMS + 3, tm=tm)
    hp = _ffn(hp, gains, *ffn_w[3], tm=tm)
    new_conv_p = u_p.reshape(n_seq, seq_len, d)[None, :, seq_len - hist:, :]

    return (hp.reshape(n_seq, seq_len, d), hs.reshape(bs, ts, d),
            new_conv_p,
            k_p.reshape(n_seq, seq_len, n_heads, dh), v_p.reshape(n_seq, seq_len, n_heads, dh),
            lf_p.reshape(n_seq, seq_len, n_heads),
            new_conv_s,
            k_s.reshape(bs, ts, n_heads, dh), v_s.reshape(bs, ts, n_heads, dh),
            lf_s.reshape(bs, ts, n_heads))
```

```python
import functools
import math

import jax
import jax.numpy as jnp
from jax import lax
from jax.experimental import pallas as pl
from jax.experimental.pallas import tpu as pltpu

F32 = jnp.float32
BF16 = jnp.bfloat16

RMS_EPS = 1e-6
LN_EPS = 1e-5
NEG_INF = -1e30
LOG2E = math.log2(math.e)
N_NORMS = 6
LANES = 128
SUBLANES = 8
PAGES_PER_STEP = 8

VMEM_LIMIT_BYTES = 56 * 1024 * 1024

TM_PROMPT = 1024
TF_FFN = 256
TN_PROJ = 512
TK_RESID = 512
TT_CONV = 256
TQ_ATTN = 512
TC_CUMSUM = 256

_NT = (((1,), (1,)), ((), ()))
_TN = (((0,), (0,)), ((), ()))


def _params(semantics):
    return pltpu.CompilerParams(dimension_semantics=semantics, vmem_limit_bytes=VMEM_LIMIT_BYTES)


def _rms(x, g):
    ms = jnp.mean(x * x, axis=-1, keepdims=True)
    return (x * lax.rsqrt(ms + RMS_EPS)) * g


def _dot(a, b):
    return jnp.dot(a, b, preferred_element_type=F32)


def _split3(x):
    hi = x.astype(BF16)
    r1 = x - hi.astype(F32)
    mid = r1.astype(BF16)
    lo = (r1 - mid.astype(F32)).astype(BF16)
    return hi, mid, lo


def _ones_where(mask):
    return jnp.where(mask, 1.0, 0.0).astype(BF16)


def _lane_tile(x, reps):
    return jnp.concatenate([x] * reps, axis=1)


def _ffn_body(x_ref, gpre_ref, gpost_ref, wg_ref, wu_ref, wd_ref, o_ref, xn_ref, *, d_ff, tf):
    f = pl.program_id(1)

    @pl.when(f == 0)
    def _():
        xn_ref[...] = _rms(x_ref[...], gpre_ref[...]).astype(BF16)
        o_ref[...] = jnp.zeros_like(o_ref)

    xn = xn_ref[...]
    gate = _dot(xn, wg_ref[...].astype(BF16))
    up = _dot(xn, wu_ref[...].astype(BF16))
    act = gate * jax.nn.sigmoid(gate) * up
    col = f * tf + lax.broadcasted_iota(jnp.int32, (1, tf), 1)
    act = jnp.where(col < d_ff, act, 0.0).astype(BF16)
    row = f * tf + lax.broadcasted_iota(jnp.int32, (tf, 1), 0)
    wd = jnp.where(row < d_ff, wd_ref[...], 0.0).astype(BF16)
    o_ref[...] += _dot(act, wd)

    @pl.when(f == pl.num_programs(1) - 1)
    def _():
        o_ref[...] = x_ref[...] + 0.5 * _rms(o_ref[...], gpost_ref[...])


def _ffn_specs(d, tm, tf, n_pre, n_post, layer):
    one = pl.Buffered(1)
    in_specs = [
        pl.BlockSpec((tm, d), lambda i, f, *_: (i, 0)),
        pl.BlockSpec((None, 1, d), lambda i, f, *_: (n_pre, 0, 0)),
        pl.BlockSpec((None, 1, d), lambda i, f, *_: (n_post, 0, 0)),
        pl.BlockSpec((None, d, tf), lambda i, f, *_: (layer, 0, f)),
        pl.BlockSpec((None, d, tf), lambda i, f, *_: (layer, 0, f)),
        pl.BlockSpec((None, tf, d), lambda i, f, *_: (layer, f, 0)),
    ]
    return in_specs, pl.BlockSpec((tm, d), lambda i, f, *_: (i, 0), pipeline_mode=one)


def _ffn(h, gains, n_pre, n_post, w_gate, w_up, w_down, layer, *, tm):
    m, d = h.shape
    d_ff = w_gate.shape[-1]
    tf = TF_FFN
    in_specs, out_spec = _ffn_specs(d, tm, tf, n_pre, n_post, layer)
    return pl.pallas_call(
        functools.partial(_ffn_body, d_ff=d_ff, tf=tf),
        out_shape=jax.ShapeDtypeStruct((m, d), F32),
        grid=(m // tm, pl.cdiv(d_ff, tf)),
        in_specs=in_specs,
        out_specs=out_spec,
        scratch_shapes=[pltpu.VMEM((tm, d), BF16)],
        compiler_params=_params(("parallel", "arbitrary")),
        name="ffn",
    )(h, gains, gains, w_gate, w_up, w_down)


def _glu_body(x_ref, g_ref, wa_ref, wg_ref, ba_ref, bg_ref, o_ref, xn_ref):
    @pl.when(pl.program_id(1) == 0)
    def _():
        xn_ref[...] = _rms(x_ref[...], g_ref[...]).astype(BF16)

    xn = xn_ref[...]
    a = _dot(xn, wa_ref[...].astype(BF16)) + ba_ref[...]
    gate = _dot(xn, wg_ref[...].astype(BF16)) + bg_ref[...]
    o_ref[...] = a * jax.nn.sigmoid(gate)


def _glu_proj(h, gains, n_gain, w_pw1, b_pw1, *, tm):
    m, d = h.shape
    dc = w_pw1.shape[-1] // 2
    tn = TN_PROJ
    nj = dc // tn
    b3 = b_pw1.reshape(b_pw1.shape[0], 1, 2 * dc)
    return pl.pallas_call(
        _glu_body,
        out_shape=jax.ShapeDtypeStruct((m, dc), F32),
        grid=(m // tm, nj),
        in_specs=[
            pl.BlockSpec((tm, d), lambda i, j: (i, 0)),
            pl.BlockSpec((None, 1, d), lambda i, j: (n_gain, 0, 0)),
            pl.BlockSpec((None, d, tn), lambda i, j: (0, 0, j)),
            pl.BlockSpec((None, d, tn), lambda i, j: (0, 0, j + nj)),
            pl.BlockSpec((None, 1, tn), lambda i, j: (0, 0, j)),
            pl.BlockSpec((None, 1, tn), lambda i, j: (0, 0, j + nj)),
        ],
        out_specs=pl.BlockSpec((tm, tn), lambda i, j: (i, j)),
        scratch_shapes=[pltpu.VMEM((tm, d), BF16)],
        compiler_params=_params(("parallel", "arbitrary")),
        name="glu_proj",
    )(h, gains, w_pw1, w_pw1, b3, b3)


def _kv_body(x_ref, g_ref, wk_ref, wv_ref, wf_ref, bf_ref, k_ref, v_ref, lf_ref, *rest, blocked):
    xn_ref = rest[-1]

    @pl.when(pl.program_id(1) == 0)
    def _():
        xn = _rms(x_ref[...], g_ref[...]).astype(BF16)
        xn_ref[...] = xn
        z = _dot(xn, wf_ref[...].astype(BF16)) + bf_ref[...]
        lf_ref[...] = jnp.minimum(z, 0.0) - jnp.log1p(jnp.exp(-jnp.abs(z)))

    xn = xn_ref[...]
    k = _dot(xn, wk_ref[...].astype(BF16))
    v = _dot(xn, wv_ref[...].astype(BF16))
    k_ref[...] = k
    v_ref[...] = v
    if blocked:
        kb_ref, vb_ref = rest[0], rest[1]
        for c in range(kb_ref.shape[0]):
            cs = slice(c * LANES, (c + 1) * LANES)
            kb_ref[c] = k[:, cs].astype(BF16)
            vb_ref[c] = v[:, cs].astype(BF16)


def _kv_proj(h, g, w_k, w_v, w_f, b_f, *, tm, blocked):
    m, d = h.shape
    da = w_k.shape[-1]
    nh = w_f.shape[-1]
    tn = TN_PROJ
    nc = tn // LANES
    out_shape = [jax.ShapeDtypeStruct((m, da), F32), jax.ShapeDtypeStruct((m, da), F32),
                 jax.ShapeDtypeStruct((m, nh), F32)]
    out_specs = [pl.BlockSpec((tm, tn), lambda i, j: (i, j)),
                 pl.BlockSpec((tm, tn), lambda i, j: (i, j)),
                 pl.BlockSpec((tm, nh), lambda i, j: (i, 0))]
    if blocked:
        out_shape += [jax.ShapeDtypeStruct((da // LANES, m, LANES), BF16)] * 2
        out_specs += [pl.BlockSpec((nc, tm, LANES), lambda i, j: (j, i, 0))] * 2
    return pl.pallas_call(
        functools.partial(_kv_body, blocked=blocked),
        out_shape=tuple(out_shape),
        grid=(m // tm, da // tn),
        in_specs=[
            pl.BlockSpec((tm, d), lambda i, j: (i, 0)),
            pl.BlockSpec((1, d), lambda i, j: (0, 0)),
            pl.BlockSpec((d, tn), lambda i, j: (0, j)),
            pl.BlockSpec((d, tn), lambda i, j: (0, j)),
            pl.BlockSpec((d, nh), lambda i, j: (0, 0)),
            pl.BlockSpec((1, nh), lambda i, j: (0, 0)),
        ],
        out_specs=tuple(out_specs),
        scratch_shapes=[pltpu.VMEM((tm, d), BF16)],
        compiler_params=_params(("parallel", "arbitrary")),
        name="kv_proj",
    )(h, g.reshape(1, d), w_k, w_v, w_f, b_f.reshape(1, nh))


def _q_body(x_ref, g_ref, w_ref, o_ref, xn_ref):
    @pl.when(pl.program_id(1) == 0)
    def _():
        xn_ref[...] = _rms(x_ref[...], g_ref[...]).astype(BF16)

    q = _dot(xn_ref[...], w_ref[...].astype(BF16))
    for c in range(o_ref.shape[0]):
        o_ref[c] = q[:, c * LANES:(c + 1) * LANES].astype(o_ref.dtype)


def _q_proj(h, gains, n_gain, w_q, *, tm):
    m, d = h.shape
    da = w_q.shape[-1]
    tn = TN_PROJ
    nc = tn // LANES
    return pl.pallas_call(
        _q_body,
        out_shape=jax.ShapeDtypeStruct((da // LANES, m, LANES), BF16),
        grid=(m // tm, da // tn),
        in_specs=[
            pl.BlockSpec((tm, d), lambda i, j: (i, 0)),
            pl.BlockSpec((None, 1, d), lambda i, j: (n_gain, 0, 0)),
            pl.BlockSpec((None, d, tn), lambda i, j: (0, 0, j)),
        ],
        out_specs=pl.BlockSpec((nc, tm, LANES), lambda i, j: (j, i, 0)),
        scratch_shapes=[pltpu.VMEM((tm, d), BF16)],
        compiler_params=_params(("parallel", "arbitrary")),
        name="q_proj",
    )(h, gains, w_q)


def _resid_mm_body(h_ref, y_ref, w_ref, b_ref, g_ref, o_ref):
    k = pl.program_id(1)

    @pl.when(k == 0)
    def _():
        o_ref[...] = jnp.broadcast_to(b_ref[...], o_ref.shape)

    y = jnp.concatenate([y_ref[c] for c in range(y_ref.shape[0])], axis=1)
    o_ref[...] += _dot(y, w_ref[...].astype(BF16))

    @pl.when(k == pl.num_programs(1) - 1)
    def _():
        o_ref[...] = h_ref[...] + _rms(o_ref[...], g_ref[...])


def _resid_mm(h, y, w, b, gains, n_gain, *, tm):
    m, d = h.shape
    tk = TK_RESID
    nc = tk // LANES
    return pl.pallas_call(
        _resid_mm_body,
        out_shape=jax.ShapeDtypeStruct((m, d), F32),
        grid=(m // tm, y.shape[0] // nc),
        in_specs=[
            pl.BlockSpec((tm, d), lambda i, k: (i, 0)),
            pl.BlockSpec((nc, tm, LANES), lambda i, k: (k, i, 0)),
            pl.BlockSpec((None, tk, d), lambda i, k: (0, k, 0)),
            pl.BlockSpec((1, d), lambda i, k: (0, 0)),
            pl.BlockSpec((None, 1, d), lambda i, k: (n_gain, 0, 0)),
        ],
        out_specs=pl.BlockSpec((tm, d), lambda i, k: (i, 0)),
        compiler_params=_params(("parallel", "arbitrary")),
        name="resid_mm",
    )(h, y, w, b.reshape(1, d), gains)


def _ln_silu(y, g, b):
    mu = jnp.mean(y, axis=-1, keepdims=True)
    dlt = y - mu
    var = jnp.mean(dlt * dlt, axis=-1, keepdims=True)
    z = (dlt * lax.rsqrt(var + LN_EPS)) * g + b
    return z * jax.nn.sigmoid(z)


_CONV_COLS = 512
_HALO = 32


def _conv_prompt_body(halo_ref, u_ref, wdw_ref, bdw_ref, lng_ref, lnb_ref, o_ref, ubuf_ref, ybuf_ref,
                      wb_ref, *, tt, width):
    i = pl.program_id(1)
    d = u_ref.shape[-1]
    ubuf_ref[pl.ds(0, _HALO), :] = jnp.where(i > 0, halo_ref[...], 0.0)
    ubuf_ref[pl.ds(_HALO, tt), :] = u_ref[...]
    for w in range(width):
        wb_ref[w] = jnp.broadcast_to(wdw_ref[w:w + 1, :], (SUBLANES, d))
    off = _HALO - (width - 1)
    n_tiles = (off + width - 1 + SUBLANES - 1) // SUBLANES + 1

    def conv_chunk(r, carry):
        base = pl.multiple_of(r * SUBLANES, SUBLANES)
        sub = lax.broadcasted_iota(jnp.int32, (SUBLANES, _CONV_COLS), 0)
        for cb in range(d // _CONV_COLS):
            cs = slice(cb * _CONV_COLS, (cb + 1) * _CONV_COLS)
            tiles = [ubuf_ref[pl.ds(base + SUBLANES * a, SUBLANES), cs] for a in range(n_tiles)]
            acc = jnp.broadcast_to(bdw_ref[:, cs], (SUBLANES, _CONV_COLS))
            for s in range(SUBLANES):
                z = None
                for a in range(n_tiles):
                    w = SUBLANES * a + s - off
                    if not 0 <= w < width:
                        continue
                    src = tiles[a] if s == 0 else jnp.where(sub >= s, tiles[a], tiles[a + 1])
                    term = src * wb_ref[w, :, cs]
                    z = term if z is None else z + term
                acc = acc + (z if s == 0 else pltpu.roll(z, SUBLANES - s, 0))
            ybuf_ref[pl.ds(base, SUBLANES), cs] = acc
        return carry

    lax.fori_loop(0, tt // SUBLANES, conv_chunk, 0)

    def ln_chunk(r, carry):
        base = pl.multiple_of(r * 16, 16)
        y = _ln_silu(ybuf_ref[pl.ds(base, 16), :], lng_ref[...], lnb_ref[...]).astype(o_ref.dtype)
        for c in range(o_ref.shape[0]):
            o_ref[c, pl.ds(base, 16), :] = y[:, c * LANES:(c + 1) * LANES]
        return carry

    lax.fori_loop(0, tt // 16, ln_chunk, 0, unroll=2)


def _conv_prompt(u, n_seq, seq_len, w_dw, b_dw, ln_g, ln_b):
    d = u.shape[-1]
    width = w_dw.shape[1]
    tt = TT_CONV
    nt = seq_len // tt
    hb = tt // _HALO
    return pl.pallas_call(
        functools.partial(_conv_prompt_body, tt=tt, width=width),
        out_shape=jax.ShapeDtypeStruct((d // LANES, n_seq * seq_len, LANES), BF16),
        grid=(n_seq, nt),
        in_specs=[
            pl.BlockSpec((_HALO, d), lambda b, i: (jnp.maximum((b * nt + i) * hb - 1, 0), 0)),
            pl.BlockSpec((tt, d), lambda b, i: (b * nt + i, 0)),
            pl.BlockSpec((None, width, d), lambda b, i: (0, 0, 0)),
            pl.BlockSpec((1, d), lambda b, i: (0, 0)),
            pl.BlockSpec((1, d), lambda b, i: (0, 0)),
            pl.BlockSpec((1, d), lambda b, i: (0, 0)),
        ],
        out_specs=pl.BlockSpec((d // LANES, tt, LANES), lambda b, i: (0, b * nt + i, 0)),
        scratch_shapes=[pltpu.VMEM((tt + _HALO, d), F32), pltpu.VMEM((tt, d), F32),
                        pltpu.VMEM((width, SUBLANES, d), F32)],
        compiler_params=_params(("parallel", "parallel")),
        name="conv_prompt",
    )(u, u, w_dw, b_dw, ln_g, ln_b)


def _conv_sample_body(st_ref, u_ref, wdw_ref, bdw_ref, lng_ref, lnb_ref, y_ref, ns_ref, ubuf_ref,
                      *, ts, width):
    hist = width - 1
    ubuf_ref[pl.ds(0, hist), :] = st_ref[...]
    ubuf_ref[pl.ds(hist, ts), :] = u_ref[...]
    full = ubuf_ref[...]
    acc = jnp.broadcast_to(bdw_ref[...], (ts, full.shape[-1]))
    for w in range(width):
        acc = acc + full[w:w + ts] * wdw_ref[w:w + 1, :]
    y_ref[...] = _ln_silu(acc, lng_ref[...], lnb_ref[...])
    ns_ref[...] = full[ts:ts + hist]


def _conv_sample(u_s, state, w_dw, b_dw, ln_g, ln_b):
    bs, ts, d = u_s.shape
    width = w_dw.shape[1]
    hist = width - 1
    return pl.pallas_call(
        functools.partial(_conv_sample_body, ts=ts, width=width),
        out_shape=(jax.ShapeDtypeStruct((bs, ts, d), F32),
                   jax.ShapeDtypeStruct((1, bs, hist, d), F32)),
        grid=(bs,),
        in_specs=[
            pl.BlockSpec((None, None, hist, d), lambda b: (0, b, 0, 0)),
            pl.BlockSpec((None, ts, d), lambda b: (b, 0, 0)),
            pl.BlockSpec((None, width, d), lambda b: (0, 0, 0)),
            pl.BlockSpec((1, d), lambda b: (0, 0)),
            pl.BlockSpec((1, d), lambda b: (0, 0)),
            pl.BlockSpec((1, d), lambda b: (0, 0)),
        ],
        out_specs=(pl.BlockSpec((None, ts, d), lambda b: (b, 0, 0)),
                   pl.BlockSpec((None, None, hist, d), lambda b: (0, b, 0, 0))),
        scratch_shapes=[pltpu.VMEM((hist + ts, d), F32)],
        compiler_params=_params(("parallel",)),
        name="conv_sample",
    )(state, u_s, w_dw, b_dw, ln_g, ln_b)


def _cumsum_body(x_ref, ct_ref, carry_ref, *, tc):
    @pl.when(pl.program_id(1) == 0)
    def _():
        carry_ref[...] = jnp.zeros_like(carry_ref)

    row = lax.broadcasted_iota(jnp.int32, (tc, tc), 0)
    col = lax.broadcasted_iota(jnp.int32, (tc, tc), 1)
    upper = _ones_where(row <= col)
    ctb = carry_ref[...]
    for piece in _split3(x_ref[...]):
        ctb = ctb + lax.dot_general(piece, upper, _TN, preferred_element_type=F32)
    for h in range(ct_ref.shape[0]):
        ct_ref[h] = ctb[h:h + 1, :]
    carry_ref[...] = ctb[:, tc - 1:tc]


def _cumsum_prompt(logf, n_seq, seq_len):
    nh = logf.shape[-1]
    tc = TC_CUMSUM
    nt = seq_len // tc
    return pl.pallas_call(
        functools.partial(_cumsum_body, tc=tc),
        out_shape=jax.ShapeDtypeStruct((n_seq, nh, 1, seq_len), F32),
        grid=(n_seq, nt),
        in_specs=[pl.BlockSpec((tc, nh), lambda b, j: (b * nt + j, 0))],
        out_specs=pl.BlockSpec((None, nh, 1, tc), lambda b, j: (b, 0, 0, j)),
        scratch_shapes=[pltpu.VMEM((nh, 1), F32)],
        compiler_params=_params(("parallel", "arbitrary")),
        name="cumsum_prompt",
    )(logf)


def _attn_prompt_body(q_ref, k_ref, v_ref, ctq_ref, ctk_ref, o_ref, m_ref, l_ref, acc_ref, cq_ref,
                      *, scale2):
    qi = pl.program_id(1)
    ki = pl.program_id(2)
    n_heads, tq, dh = q_ref.shape
    tk = k_ref.shape[1]
    reps = tk // dh

    @pl.when(ki == 0)
    def _():
        m_ref[...] = jnp.full_like(m_ref, NEG_INF)
        l_ref[...] = jnp.zeros_like(l_ref)
        acc_ref[...] = jnp.zeros_like(acc_ref)

        def init_head(h, carry):
            row = ctq_ref[h] * LOG2E
            cq_ref[h] = jnp.broadcast_to(row, (dh, tq)).T
            return carry

        lax.fori_loop(0, n_heads, init_head, 0)

    def run(diagonal):
        def head(h, carry):
            s = lax.dot_general(q_ref[h], k_ref[h], _NT, preferred_element_type=F32) * scale2
            s = s + _lane_tile(cq_ref[h], reps) - ctk_ref[h] * LOG2E
            if diagonal:
                row = lax.broadcasted_iota(jnp.int32, (tq, tk), 0)
                col = lax.broadcasted_iota(jnp.int32, (tq, tk), 1)
                s = jnp.where(col <= row, s, NEG_INF)
            m_prev = m_ref[h]
            m_new = jnp.maximum(m_prev, jnp.max(s, axis=-1, keepdims=True))
            alpha = jnp.exp2(m_prev - m_new)
            p = jnp.exp2(s - _lane_tile(m_new, reps))
            l_ref[h] = alpha * l_ref[h] + jnp.sum(p, axis=-1, keepdims=True)
            acc_ref[h] = alpha * acc_ref[h] + _dot(p.astype(BF16), v_ref[h])
            m_ref[h] = m_new
            return carry

        lax.fori_loop(0, n_heads, head, 0, unroll=4)

    @pl.when(ki < qi)
    def _():
        run(False)

    @pl.when(ki == qi)
    def _():
        run(True)

        def finish(h, carry):
            o_ref[h] = (acc_ref[h] / l_ref[h]).astype(o_ref.dtype)
            return carry

        lax.fori_loop(0, n_heads, finish, 0)


def _attn_prompt(q, k, v, ct, n_seq, seq_len):
    n_heads, _, dh = q.shape
    tq = TQ_ATTN
    nq = seq_len // tq
    stat = pltpu.VMEM((n_heads, tq, dh), F32)
    kv_map = lambda b, qi, ki: (0, b * nq + jnp.minimum(ki, qi), 0)
    return pl.pallas_call(
        functools.partial(_attn_prompt_body, scale2=dh ** -0.5 * LOG2E),
        out_shape=jax.ShapeDtypeStruct((n_heads, n_seq * seq_len, dh), BF16),
        grid=(n_seq, nq, nq),
        in_specs=[
            pl.BlockSpec((n_heads, tq, dh), lambda b, qi, ki: (0, b * nq + qi, 0)),
            pl.BlockSpec((n_heads, tq, dh), kv_map),
            pl.BlockSpec((n_heads, tq, dh), kv_map),
            pl.BlockSpec((None, n_heads, 1, tq), lambda b, qi, ki: (b, 0, 0, qi)),
            pl.BlockSpec((None, n_heads, 1, tq), lambda b, qi, ki: (b, 0, 0, jnp.minimum(ki, qi))),
        ],
        out_specs=pl.BlockSpec((n_heads, tq, dh), lambda b, qi, ki: (0, b * nq + qi, 0)),
        scratch_shapes=[stat, stat, stat, stat],
        compiler_params=_params(("parallel", "parallel", "arbitrary")),
        name="attn_prompt",
    )(q, k, v, ct, ct)


def _lane_iota(shape):
    return lax.broadcasted_iota(jnp.int32, shape, len(shape) - 1)


def _suffix_sums_page(x, n_heads):
    n_rows, n_lanes = x.shape
    lane = _lane_iota(x.shape)
    sub = lax.broadcasted_iota(jnp.int32, x.shape, 0)
    y = x
    sh = n_heads
    while sh < n_lanes:
        y = y + jnp.where(lane + sh < n_lanes, pltpu.roll(y, n_lanes - sh, 1), 0.0)
        sh *= 2
    z = jnp.where(lane < n_heads, y, 0.0)
    sh = n_heads
    while sh < n_lanes:
        z = z + pltpu.roll(z, sh, 1)
        sh *= 2
    zi = z
    sh = 1
    while sh < n_rows:
        zi = zi + jnp.where(sub + sh < n_rows, pltpu.roll(zi, n_rows - sh, 0), 0.0)
        sh *= 2
    return y + (zi - z), zi[0:1, :]


def _attn_sample_parts(q_ref, kn_ref, vn_ref, lfn_ref, *refs, n_heads, ts, scale2):
    npg = PAGES_PER_STEP
    k_refs = refs[0:npg]
    v_refs = refs[npg:2 * npg]
    lf_refs = refs[2 * npg:3 * npg]
    o_ref, mask_ref, m_ref, l_ref, acc_ref, tail_ref, cq_ref = refs[3 * npg:]
    rows, dh = q_ref.shape
    page, n_groups, hg, _ = k_refs[0].shape
    rg = rows // n_groups
    n_keys = page * hg

    def attend(s, v_bf16, state):
        rmax = jnp.max(s, axis=-1, keepdims=True)
        m_new = rmax if state is None else jnp.maximum(state[0], rmax)
        p = jnp.exp2(s - m_new)
        psum = jnp.sum(p, axis=-1, keepdims=True)
        pv = _dot(p.astype(BF16), v_bf16)
        if state is None:
            return m_new, psum, pv
        alpha = jnp.exp2(state[0] - m_new)
        return m_new, alpha * state[1] + psum, alpha * state[2] + pv

    def init():
        r_i = lax.broadcasted_iota(jnp.int32, (rg, n_keys), 0)
        l_i = _lane_iota((rg, n_keys))
        mask_ref[...] = jnp.where(l_i % hg == r_i // ts, 0.0, NEG_INF)

        n_new = ts * n_heads
        cn = jnp.broadcast_to(lfn_ref[...], (8, n_new))
        lane8 = _lane_iota((8, n_new))
        sh = n_heads
        while sh < n_new:
            cn = cn + jnp.where(lane8 >= sh, pltpu.roll(cn, sh, 1), 0.0)
            sh *= 2
        cn_row = cn[0:1, :]
        r2 = lax.broadcasted_iota(jnp.int32, (rows, n_new), 0)
        l2 = _lane_iota((rows, n_new))
        own = l2 == (r2 % ts) * n_heads + r2 // ts
        cq = jnp.sum(jnp.where(own, jnp.broadcast_to(cn_row, (rows, n_new)), 0.0),
                     axis=-1, keepdims=True)
        cq_ref[...] = cq * LOG2E
        tail_ref[...] = jnp.zeros_like(tail_ref)

        s = lax.dot_general(q_ref[...], kn_ref[...].astype(BF16), _NT,
                            preferred_element_type=F32) * scale2
        s = s + (cq - cn_row) * LOG2E
        valid = (l2 % n_heads == r2 // ts) & (l2 // n_heads <= r2 % ts)
        s = jnp.where(valid, s, NEG_INF)
        m, l, acc = attend(s, vn_ref[...].astype(BF16), None)
        m_ref[...] = m
        l_ref[...] = l
        acc_ref[...] = acc

    def pages():
        for g in range(n_groups):
            rs = pl.ds(g * rg, rg)
            tail = tail_ref[g, 0:1, :]
            e_tiles = [None] * npg
            for i in reversed(range(npg)):
                x = lf_refs[i][g]
                incl, tot = _suffix_sums_page(x, hg)
                e_tiles[i] = (incl - x + tail) * LOG2E
                tail = tail + tot
            tail_ref[g] = jnp.broadcast_to(tail, tail_ref.shape[1:])
            e_row = jnp.concatenate([e[a:a + 1, :] for e in e_tiles for a in range(e.shape[0])],
                                    axis=1)
            bias = jnp.concatenate([cq_ref[rs, :] + mask_ref[...]] * npg, axis=1) + e_row

            k_g = jnp.concatenate([r[:, g].reshape(n_keys, dh).astype(BF16) for r in k_refs], axis=0)
            v_g = jnp.concatenate([r[:, g].reshape(n_keys, dh).astype(BF16) for r in v_refs], axis=0)
            s = lax.dot_general(q_ref[rs, :], k_g, _NT, preferred_element_type=F32) * scale2
            state = attend(s + bias, v_g, (m_ref[rs, :], l_ref[rs, :], acc_ref[rs, :]))
            m_ref[rs, :], l_ref[rs, :], acc_ref[rs, :] = state

    def finish():
        o_ref[...] = acc_ref[...] / l_ref[...]

    return init, pages, finish


def _attn_sample_body(pt_ref, *refs, n_heads, ts, scale2):
    del pt_ref
    init, pages, finish = _attn_sample_parts(*refs, n_heads=n_heads, ts=ts, scale2=scale2)
    j = pl.program_id(1)
    pl.when(j == 0)(init)
    pages()
    pl.when(j == pl.num_programs(1) - 1)(finish)


def _attn_sample(q2, kn2, vn2, lfn, cache_k, cache_v, cache_logf, page_table, n_heads, ts):
    bs, rows, dh = q2.shape
    n_phys, page = cache_k.shape[0], cache_k.shape[1]
    npg = PAGES_PER_STEP
    n_steps = page_table.shape[1] // npg
    hg = SUBLANES
    n_groups = n_heads // hg
    n_keys = page * hg
    lanes = kn2.shape[1]
    ck = cache_k.reshape(n_phys, page, n_groups, hg, dh)
    cv = cache_v.reshape(n_phys, page, n_groups, hg, dh)
    lfc = cache_logf.reshape(n_phys, page, n_groups, hg).transpose(0, 2, 1, 3).reshape(
        n_phys, n_groups, n_keys // lanes, lanes)

    def page_map(p, nd):
        return lambda b, j, pt: (pt[b, (n_steps - 1 - j) * npg + p],) + (0,) * nd

    per_seq = lambda b, j, pt: (b, 0, 0)
    in_specs = [pl.BlockSpec((None, rows, dh), per_seq),
                pl.BlockSpec((None, lanes, dh), per_seq),
                pl.BlockSpec((None, lanes, dh), per_seq),
                pl.BlockSpec((None, 1, lanes), per_seq)]
    in_specs += [pl.BlockSpec((None, page, n_groups, hg, dh), page_map(p, 4)) for p in range(npg)]
    in_specs += [pl.BlockSpec((None, page, n_groups, hg, dh), page_map(p, 4)) for p in range(npg)]
    in_specs += [pl.BlockSpec((None, n_groups, n_keys // lanes, lanes), page_map(p, 3))
                 for p in range(npg)]
    grid_spec = pltpu.PrefetchScalarGridSpec(
        num_scalar_prefetch=1,
        grid=(bs, n_steps),
        in_specs=in_specs,
        out_specs=pl.BlockSpec((None, rows, dh), per_seq),
        scratch_shapes=[pltpu.VMEM((rows // n_groups, n_keys), F32), pltpu.VMEM((rows, 1), F32),
                        pltpu.VMEM((rows, 1), F32), pltpu.VMEM((rows, dh), F32),
                        pltpu.VMEM((n_groups, SUBLANES, lanes), F32), pltpu.VMEM((rows, 1), F32)],
    )
    return pl.pallas_call(
        functools.partial(_attn_sample_body, n_heads=n_heads, ts=ts, scale2=dh ** -0.5 * LOG2E),
        out_shape=jax.ShapeDtypeStruct((bs, rows, dh), F32),
        grid_spec=grid_spec,
        compiler_params=_params(("parallel", "arbitrary")),
        name="attn_sample",
    )(page_table, q2, kn2, vn2, lfn, *([ck] * npg), *([cv] * npg), *([lfc] * npg))


def _to_blocks(x):
    rows, c = x.shape
    return x.reshape(rows, c // LANES, LANES).transpose(1, 0, 2)


def kernel(x_prompt, x_sample, state_conv, cache_k, cache_v, cache_logf, page_table, norm_gain,
           ffn1_w_gate, ffn1_w_up, ffn1_w_down, ffn2_w_gate, ffn2_w_up, ffn2_w_down,
           conv_w_pw1, conv_b_pw1, conv_w_dw, conv_b_dw, conv_ln_g, conv_ln_b, conv_w_pw2,
           conv_b_pw2, kv_norm_g, w_k, w_v, w_fgate, b_fgate, attn_w_q, attn_w_o):
    n_seq, seq_len, d = x_prompt.shape
    bs, ts, _ = x_sample.shape
    depth = norm_gain.shape[0]
    n_heads = w_fgate.shape[-1]
    dh = w_k.shape[-1] // n_heads
    hist = conv_w_dw.shape[1] - 1
    assert depth == 2 and conv_w_pw1.shape[0] == 1 and attn_w_q.shape[0] == 1
    assert seq_len >= hist and ts * n_heads == LANES and dh == LANES
    mp, ms = n_seq * seq_len, bs * ts
    gains = norm_gain.reshape(depth * N_NORMS, 1, d)
    zero_bias = jnp.zeros((d,), F32)

    ffn_w = {
        0: (0, 1, ffn1_w_gate, ffn1_w_up, ffn1_w_down, 0),
        1: (4, 5, ffn2_w_gate, ffn2_w_up, ffn2_w_down, 0),
        2: (N_NORMS + 0, N_NORMS + 1, ffn1_w_gate, ffn1_w_up, ffn1_w_down, 1),
        3: (N_NORMS + 4, N_NORMS + 5, ffn2_w_gate, ffn2_w_up, ffn2_w_down, 1),
    }

    tm_s = ms
    hs = _ffn(x_sample.reshape(ms, d), gains, *ffn_w[0], tm=tm_s)
    u_s = _glu_proj(hs, gains, 2, conv_w_pw1, conv_b_pw1, tm=tm_s)
    y_s, new_conv_s = _conv_sample(u_s.reshape(bs, ts, d), state_conv, conv_w_dw, conv_b_dw,
                                   conv_ln_g, conv_ln_b)
    hs = _resid_mm(hs, _to_blocks(y_s.reshape(ms, d)).astype(BF16), conv_w_pw2, conv_b_pw2[0],
                   gains, 3, tm=tm_s)
    hs = _ffn(hs, gains, *ffn_w[1], tm=tm_s)
    k_s, v_s, lf_s = _kv_proj(hs, kv_norm_g, w_k, w_v, w_fgate, b_fgate, tm=tm_s, blocked=False)
    hs = _ffn(hs, gains, *ffn_w[2], tm=tm_s)
    q_s = _q_proj(hs, gains, N_NORMS + 2, attn_w_q, tm=tm_s)
    q2 = q_s.reshape(n_heads, bs, ts, dh).transpose(1, 0, 2, 3).reshape(bs, n_heads * ts, dh)
    o2 = _attn_sample(q2, k_s.reshape(bs, ts * n_heads, dh), v_s.reshape(bs, ts * n_heads, dh),
                      lf_s.reshape(bs, 1, ts * n_heads), cache_k, cache_v, cache_logf,
                      page_table, n_heads, ts)
    o_s = o2.reshape(bs, n_heads, ts, dh).transpose(1, 0, 2, 3).reshape(n_heads, ms, dh)
    hs = _resid_mm(hs, o_s.astype(BF16), attn_w_o, zero_bias, gains, N_NORMS + 3, tm=tm_s)
    hs = _ffn(hs, gains, *ffn_w[3], tm=tm_s)

    tm = TM_PROMPT
    hp = _ffn(x_prompt.reshape(mp, d), gains, *ffn_w[0], tm=tm)
    u_p = _glu_proj(hp, gains, 2, conv_w_pw1, conv_b_pw1, tm=tm)
    y_p = _conv_prompt(u_p, n_seq, seq_len, conv_w_dw, conv_b_dw, conv_ln_g, conv_ln_b)
    hp = _resid_mm(hp, y_p, conv_w_pw2, conv_b_pw2[0], gains, 3, tm=tm)
    hp = _ffn(hp, gains, *ffn_w[1], tm=tm)
    k_p, v_p, lf_p, kb_p, vb_p = _kv_proj(hp, kv_norm_g, w_k, w_v, w_fgate, b_fgate, tm=tm,
                                          blocked=True)
    hp = _ffn(hp, gains, *ffn_w[2], tm=tm)
    q_p = _q_proj(hp, gains, N_NORMS + 2, attn_w_q, tm=tm)
    ct = _cumsum_prompt(lf_p, n_seq, seq_len)
    o_p = _attn_prompt(q_p, kb_p, vb_p, ct, n_seq, seq_len)
    hp = _resid_mm(hp, o_p, attn_w_o, zero_bias, gains, N_NORMS + 3, tm=tm)
    hp = _ffn(hp, gains, *ffn_w[3], tm=tm)
    new_conv_p = u_p.reshape(n_seq, seq_len, d)[None, :, seq_len - hist:, :]

    return (hp.reshape(n_seq, seq_len, d), hs.reshape(bs, ts, d),
            new_conv_p,
            k_p.reshape(n_seq, seq_len, n_heads, dh), v_p.reshape(n_seq, seq_len, n_heads, dh),
            lf_p.reshape(n_seq, seq_len, n_heads),
            new_conv_s,
            k_s.reshape(bs, ts, n_heads, dh), v_s.reshape(bs, ts, n_heads, dh),
            lf_s.reshape(bs, ts, n_heads))
```
